```python
import math
import jax, jax.numpy as jnp
from jax import lax
import numpy as np

D_MODEL = 2048
BATCH = 4
SEQ = 2048
DEPTH = 2
DEC_BATCH = 128
DEC_SEQ = 8
PAST_LEN = 2048
PAGE_SIZE = 128

LRU_WIDTH = D_MODEL // 4
LRU_BLOCKS = 8
LRU_BLOCK = LRU_WIDTH // LRU_BLOCKS
LRU_C = 8.0
CONV_W = 4
RET_HEADS = 8
RET_DIM = D_MODEL // 4 // RET_HEADS
RET_WIDTH = RET_HEADS * RET_DIM
MLSTM_HEADS = 4
MLSTM_DIM = D_MODEL // 4 // MLSTM_HEADS
MLSTM_WIDTH = MLSTM_HEADS * MLSTM_DIM
DIL_GROUPS = ((128, 1), (512, 4), (2048, 16))
DIL_SPAN = 128
DIL_HEADS_PER_GROUP = 4
ATT_DIM = 64
N_DIL_HEADS = len(DIL_GROUPS) * DIL_HEADS_PER_GROUP
DIL_WIDTH = N_DIL_HEADS * ATT_DIM
DIL_OUT = DIL_HEADS_PER_GROUP * ATT_DIM
ATT_SCALE = ATT_DIM ** -0.5
ROPE_THETA = 10000.0
N_BRANCHES = 4
D_FF = 4 * D_MODEL
CHUNK = 128
LN_EPS = 1e-5
GN_EPS = 1e-6
ALPHA = (2 * DEPTH) ** 0.25
BETA = (8 * DEPTH) ** -0.25
_IN_SEGMENTS = (LRU_WIDTH, LRU_WIDTH,
                RET_WIDTH, RET_WIDTH, RET_WIDTH, RET_WIDTH,
                MLSTM_WIDTH, MLSTM_WIDTH, MLSTM_WIDTH,
                DIL_WIDTH, DIL_WIDTH, DIL_WIDTH,
                N_BRANCHES * D_MODEL)
N_IN_COLS = sum(_IN_SEGMENTS)

kernel_name = 'hybrid_lru_ret_mlstm_dilswa_decode_step'


def _layer_norm(x, g, b):
    xf = x.astype(jnp.float32)
    mu = jnp.mean(xf, axis=-1, keepdims=True)
    var = jnp.mean(jnp.square(xf - mu), axis=-1, keepdims=True)
    return ((xf - mu) * lax.rsqrt(var + LN_EPS)).astype(x.dtype) * g + b


def _head_norm(h, g):
    N, T = h.shape[:2]
    hf = h.astype(jnp.float32)
    mu = jnp.mean(hf, axis=-1, keepdims=True)
    var = jnp.mean(jnp.square(hf - mu), axis=-1, keepdims=True)
    return ((hf - mu) * lax.rsqrt(var + GN_EPS)).reshape(N, T, -1).astype(g.dtype) * g


def _rotary(x, pos):
    half = x.shape[-1] // 2
    inv = ROPE_THETA ** (-jnp.arange(half, dtype=jnp.float32) / half)
    ang = pos.astype(jnp.float32)[:, None] * inv[None, :]
    cos = jnp.cos(ang)[None, :, None, :]
    sin = jnp.sin(ang)[None, :, None, :]
    xf = x.astype(jnp.float32)
    x1, x2 = xf[..., :half], xf[..., half:]
    return jnp.concatenate([x1 * cos - x2 * sin, x2 * cos + x1 * sin], axis=-1).astype(x.dtype)


def _causal_conv(x, buf, w, b):
    T = x.shape[1]
    xp = jnp.concatenate([buf.astype(x.dtype), x], axis=1)
    y = xp[:, 0:T] * w[0]
    for j in range(1, CONV_W):
        y = y + xp[:, j:j + T] * w[j]
    return y + b, xp[:, xp.shape[1] - (CONV_W - 1):]


def _chunks(x, chunk):
    N, T = x.shape[:2]
    return x.reshape((N, T // chunk, chunk) + x.shape[2:]).swapaxes(0, 1)


def _unchunk(x):
    x = x.swapaxes(0, 1)
    return x.reshape((x.shape[0], x.shape[1] * x.shape[2]) + x.shape[3:])


def _rglru(x, h0, wa, ba, wx, bx, lam):
    N, T, R = x.shape
    f32 = jnp.float32
    xb = x.reshape(N, T, LRU_BLOCKS, LRU_BLOCK)
    r = jax.nn.sigmoid((jnp.einsum('ntbi,bij->ntbj', xb, wa).reshape(N, T, R) + ba).astype(f32))
    i = jax.nn.sigmoid((jnp.einsum('ntbi,bij->ntbj', xb, wx).reshape(N, T, R) + bx).astype(f32))
    log_a = -LRU_C * r * jax.nn.softplus(-lam.astype(f32))
    a = jnp.exp(log_a)
    bt = jnp.sqrt(-jnp.expm1(2.0 * log_a)) * (i * x.astype(f32))
    bt = bt.at[:, 0].add(a[:, 0] * h0.astype(f32))

    def combine(left, right):
        a_l, b_l = left
        a_r, b_r = right
        return a_l * a_r, a_r * b_l + b_r

    _, h = lax.associative_scan(combine, (a, bt), axis=1)
    return h.astype(x.dtype), h[:, -1]


def _retention(q, k, v, s0, chunk):
    f32 = jnp.float32
    H, dh = q.shape[2], q.shape[3]
    q, v = q.astype(f32), v.astype(f32)
    k = k.astype(f32) * dh ** -0.5
    log_g = jnp.log1p(-jnp.exp2(-5.0 - jnp.arange(H, dtype=f32)))
    t = jnp.arange(chunk, dtype=f32)
    rel = t[:, None] - t[None, :]
    d_in = jnp.where(rel >= 0, jnp.exp(jnp.maximum(rel, 0.0) * log_g[:, None, None]), 0.0)
    d_q = jnp.exp((t[:, None] + 1.0) * log_g[None, :])
    d_k = jnp.exp((chunk - 1.0 - t[:, None]) * log_g[None, :])
    d_s = jnp.exp(chunk * log_g)

    def step(S, blk):
        qb, kb, vb = blk
        sc = jnp.einsum('nlhd,nmhd->nhlm', qb, kb) * d_in
        o = (jnp.einsum('nhlm,nmhe->nlhe', sc, vb)
             + jnp.einsum('nlhd,nhde->nlhe', qb, S) * d_q[None, :, :, None])
        S = S * d_s[:, None, None] + jnp.einsum('nlhd,nlhe->nhde', kb * d_k[None, :, :, None], vb)
        return S, o

    S, o = lax.scan(step, s0.astype(f32), (_chunks(q, chunk), _chunks(k, chunk), _chunks(v, chunk)))
    return _unchunk(o), S


def _mlstm(q, k, v, ig, fg, c0, n0, m0, chunk):
    f32 = jnp.float32
    dh = q.shape[-1]
    q, v = q.astype(f32), v.astype(f32)
    k = k.astype(f32) * dh ** -0.5
    ig = ig.astype(f32)
    lf = jax.nn.log_sigmoid(fg.astype(f32))
    causal = jnp.tril(jnp.ones((chunk, chunk), dtype=bool))

    def step(carry, blk):
        C, n, m = carry
        qb, kb, vb, ib, lfb = blk
        F = jnp.cumsum(lfb, axis=1)
        logD = F[:, :, None, :] - F[:, None, :, :] + ib[:, None, :, :]
        logD = jnp.where(causal[None, :, :, None], logD, -jnp.inf)
        log_inter = F + m[:, None, :]
        m_t = jnp.maximum(jnp.max(logD, axis=2), log_inter)
        Dm = jnp.exp(logD - m_t[:, :, None, :])
        inter = jnp.exp(log_inter - m_t)
        s = jnp.einsum('nlhd,nmhd->nlmh', qb, kb) * Dm
        num = (jnp.einsum('nlmh,nmhe->nlhe', s, vb)
               + inter[..., None] * jnp.einsum('nlhd,nhde->nlhe', qb, C))
        den = jnp.sum(s, axis=2) + inter * jnp.einsum('nlhd,nhd->nlh', qb, n)
        h = num / jnp.maximum(jnp.abs(den), jnp.exp(-m_t))[..., None]
        m_new = m_t[:, -1]
        wk = jnp.exp(F[:, -1:, :] - F + ib - m_new[:, None, :])
        dec = jnp.exp(F[:, -1] + m - m_new)
        C = dec[..., None, None] * C + jnp.einsum('nlhd,nlhe->nhde', kb * wk[..., None], vb)
        n = dec[..., None] * n + jnp.einsum('nlhd,nlh->nhd', kb, wk)
        return (C, n, m_new), h

    carry0 = (c0.astype(f32), n0.astype(f32), m0.astype(f32))
    xs = (_chunks(q, chunk), _chunks(k, chunk), _chunks(v, chunk), _chunks(ig, chunk), _chunks(lf, chunk))
    (C, n, m), h = lax.scan(step, carry0, xs)
    return _unchunk(h), C, n, m


def _banded_attention(q, k, v):
    G, M, H, dh = q.shape
    blk = DIL_SPAN
    nb = -(-M // blk)
    padw = ((0, 0), (0, nb * blk - M), (0, 0), (0, 0))
    qb = jnp.pad(q, padw).reshape(G, nb, blk, H, dh)
    kb = jnp.pad(k, padw).reshape(G, nb, blk, H, dh)
    vb = jnp.pad(v, padw).reshape(G, nb, blk, H, dh)

    def with_prev(t):
        prev = jnp.concatenate([jnp.zeros_like(t[:, :1]), t[:, :-1]], axis=1)
        return jnp.concatenate([prev, t], axis=2)

    kk, vv = with_prev(kb), with_prev(vb)
    qi = jnp.arange(blk)[:, None] + blk
    kj = jnp.arange(2 * blk)[None, :]
    rel = qi - kj
    band = (rel >= 0) & (rel <= DIL_SPAN)
    has_prev = (jnp.arange(nb)[:, None, None] > 0) | (kj[None] >= blk)
    mask = band[None] & has_prev
    s = jnp.einsum('gbqhd,gbkhd->gbhqk', qb, kk).astype(jnp.float32) * ATT_SCALE
    s = jnp.where(mask[None, :, None], s, -jnp.inf)
    lse = jax.nn.logsumexp(s, axis=-1)
    p = jnp.exp(s - lse[..., None]).astype(v.dtype)
    o = jnp.einsum('gbhqk,gbkhd->gbqhd', p, vv).reshape(G, nb * blk, H, dh)[:, :M]
    lse = lse.transpose(0, 1, 3, 2).reshape(G, nb * blk, H)[:, :M]
    return o, lse


def _dilated_prompt(q, k, v, dil):
    N, T, H, dh = q.shape
    M = T // dil

    def to_res(t):
        return t.reshape(N, M, dil, H, dh).swapaxes(1, 2).reshape(N * dil, M, H, dh)

    o, lse = _banded_attention(to_res(q), to_res(k), to_res(v))
    o = o.reshape(N, dil, M, H, dh).swapaxes(1, 2).reshape(N, T, H, dh)
    lse = lse.reshape(N, dil, M, H).swapaxes(1, 2).reshape(N, T, H)
    return o, lse


def _dilated_sample(q, k, v, buf, dil):
    S = q.shape[1]
    lb = buf.shape[1]
    k_all = jnp.concatenate([buf[:, :, 0].astype(k.dtype), k], axis=1)
    v_all = jnp.concatenate([buf[:, :, 1].astype(v.dtype), v], axis=1)
    idx = lb + jnp.arange(S)[:, None] - dil * jnp.arange(DIL_SPAN + 1)[None, :]
    valid = idx >= 0
    idx = jnp.maximum(idx, 0)
    kg = jnp.take(k_all, idx, axis=1)
    vg = jnp.take(v_all, idx, axis=1)
    s = jnp.einsum('nshd,nsjhd->nhsj', q, kg).astype(jnp.float32) * ATT_SCALE
    s = jnp.where(valid, s, -jnp.inf)
    lse = jax.nn.logsumexp(s, axis=-1)
    p = jnp.exp(s - lse[..., None]).astype(v.dtype)
    o = jnp.einsum('nhsj,nsjhd->nshd', p, vg)
    return o, lse.transpose(0, 2, 1)


def _mixer(u, p, st, pos, dil_bufs):
    N, T, _ = u.shape
    dt = u.dtype
    chunk = min(CHUNK, T)
    splits = [int(s) for s in np.cumsum(_IN_SEGMENTS)[:-1]]
    (a_x, a_gate, b_q, b_k, b_v, b_g, c_qk, c_v, c_z,
     d_q, d_k, d_v, gates) = jnp.split(u @ p['w_in'], splits, axis=-1)

    a_c, lru_conv = _causal_conv(a_x, st['lru_conv'], p['lru_conv_w'], p['lru_conv_b'])
    a_h, lru_h = _rglru(a_c, st['lru_h'], p['lru_wa'], p['lru_ba'], p['lru_wx'], p['lru_bx'], p['lru_lambda'])
    y_a = a_h * jax.nn.gelu(a_gate)

    rs = (N, T, RET_HEADS, RET_DIM)
    ret, ret_s = _retention(_rotary(b_q.reshape(rs), pos), _rotary(b_k.reshape(rs), pos),
                            b_v.reshape(rs), st['ret'], chunk)
    y_b = _head_norm(ret, p['ret_gn_g']) * jax.nn.silu(b_g)

    ms = (N, T, MLSTM_HEADS, MLSTM_DIM)
    c_conv, mlstm_conv = _causal_conv(c_qk, st['mlstm_conv'], p['mlstm_conv_w'], p['mlstm_conv_b'])
    c_act = jax.nn.silu(c_conv)
    cq = jnp.einsum('nthd,hde->nthe', c_act.reshape(ms), p['mlstm_wq'])
    ck = jnp.einsum('nthd,hde->nthe', c_act.reshape(ms), p['mlstm_wk'])
    gate_in = jnp.concatenate([cq.reshape(N, T, -1), ck.reshape(N, T, -1), c_v], axis=-1)
    gpre = gate_in @ p['mlstm_w_gates'] + p['mlstm_b_gates']
    h, mc, mn, mm = _mlstm(cq, ck, c_v.reshape(ms), gpre[..., :MLSTM_HEADS], gpre[..., MLSTM_HEADS:],
                           st['mlstm_c'], st['mlstm_n'], st['mlstm_m'], chunk)
    y_c = jax.nn.sigmoid(c_z) * (_head_norm(h, p['mlstm_gn_g']) + p['mlstm_skip'] * c_act)

    ds = (N, T, N_DIL_HEADS, ATT_DIM)
    dq = _rotary(d_q.reshape(ds), pos)
    dk = _rotary(d_k.reshape(ds), pos)
    dv = d_v.reshape(ds)
    outs, lses, kv_rows = [], [], []
    for g, (win, dil) in enumerate(DIL_GROUPS):
        hs = slice(g * DIL_HEADS_PER_GROUP, (g + 1) * DIL_HEADS_PER_GROUP)
        qg, kg, vg = dq[:, :, hs], dk[:, :, hs], dv[:, :, hs]
        rows = jnp.stack([kg, vg], axis=2)
        if dil_bufs is None:
            o, l = _dilated_prompt(qg, kg, vg, dil)
            rows = rows[:, T - min(win, T):]
        else:
            o, l = _dilated_sample(qg, kg, vg, dil_bufs[g], dil)
        outs.append(o)
        lses.append(l)
        kv_rows.append(rows)
    wts = jax.nn.softmax(jnp.stack(lses), axis=0)
    y_d = jnp.einsum('gnth,gnthd->nthd', wts.astype(dt), jnp.stack(outs)).reshape(N, T, DIL_OUT)

    g_a, g_b, g_c, g_d = jnp.split(jax.nn.sigmoid(gates), N_BRANCHES, axis=-1)
    merged = (g_a * (y_a @ p['w_br_a']) + g_b * (y_b @ p['w_br_b'])
              + g_c * (y_c @ p['w_br_c']) + g_d * (y_d @ p['w_br_d']))
    mix = merged @ p['w_out']
    new = dict(lru_conv=lru_conv, lru_h=lru_h.astype(dt), ret=ret_s.astype(dt),
               mlstm_conv=mlstm_conv, mlstm_c=mc.astype(dt), mlstm_n=mn.astype(dt),
               mlstm_m=mm.astype(dt), dil1_kv=kv_rows[0], dil4_kv=kv_rows[1], dil16_kv=kv_rows[2])
    return mix, new


def _layer(x, p, st, pos, dil_bufs):
    mix, new = _mixer(x, p, st, pos, dil_bufs)
    x = _layer_norm(ALPHA * x + mix, p['ln1_g'], p['ln1_b'])
    hid = jnp.square(jax.nn.relu(x @ p['w_up'] + p['b_up']))
    x = _layer_norm(ALPHA * x + hid @ p['w_down'] + p['b_down'], p['ln2_g'], p['ln2_b'])
    return x, new


def _zero_state(n, dtype):
    return dict(lru_conv=jnp.zeros((n, CONV_W - 1, LRU_WIDTH), dtype),
                lru_h=jnp.zeros((n, LRU_WIDTH), dtype),
                ret=jnp.zeros((n, RET_HEADS, RET_DIM, RET_DIM), dtype),
                mlstm_conv=jnp.zeros((n, CONV_W - 1, MLSTM_WIDTH), dtype),
                mlstm_c=jnp.zeros((n, MLSTM_HEADS, MLSTM_DIM, MLSTM_DIM), dtype),
                mlstm_n=jnp.zeros((n, MLSTM_HEADS, MLSTM_DIM), dtype),
                mlstm_m=jnp.zeros((n, MLSTM_HEADS), dtype))


def _stack(states, name):
    return jnp.stack([s[name] for s in states])


def setup_inputs(seed: int = 0) -> dict:
    key = jax.random.key(seed)
    ks = iter(jax.random.split(key, 64))
    f32 = jnp.float32

    def nrm(shape, scale):
        return jax.random.normal(next(ks), shape, f32) * scale

    L, D = DEPTH, D_MODEL
    Hm, dm = MLSTM_HEADS, MLSTM_DIM
    Hd = DIL_HEADS_PER_GROUP
    wl = [min(w, PAST_LEN) for w, _ in DIL_GROUPS]
    a_c = jax.random.uniform(next(ks), (L, LRU_WIDTH), f32, 0.9, 0.999)
    sig = a_c ** (1.0 / LRU_C)
    lru_lambda = jnp.log(sig) - jnp.log1p(-sig)
    b_f = jnp.broadcast_to(jnp.linspace(3.0, 6.0, Hm, dtype=f32), (L, Hm))
    mlstm_b_gates = jnp.concatenate([nrm((L, Hm), 0.1), b_f + nrm((L, Hm), 0.01)], axis=-1)
    return {
        'x_prompt': nrm((BATCH, SEQ, D), 1.0),
        'x_sample': nrm((DEC_BATCH, DEC_SEQ, D), 1.0),
        'state_lru_conv': nrm((L, DEC_BATCH, CONV_W - 1, LRU_WIDTH), 1.0),
        'state_lru_h': nrm((L, DEC_BATCH, LRU_WIDTH), 0.5),
        'state_ret': nrm((L, DEC_BATCH, RET_HEADS, RET_DIM, RET_DIM), 0.5),
        'state_mlstm_conv': nrm((L, DEC_BATCH, CONV_W - 1, MLSTM_WIDTH), 1.0),
        'state_mlstm_c': nrm((L, DEC_BATCH, Hm, dm, dm), 0.5),
        'state_mlstm_n': nrm((L, DEC_BATCH, Hm, dm), 0.5),
        'state_mlstm_m': nrm((L, DEC_BATCH, Hm), 1.0),
        'cache_dil1_kv': nrm((L, DEC_BATCH, wl[0], 2, Hd, ATT_DIM), 1.0),
        'cache_dil4_kv': nrm((L, DEC_BATCH, wl[1], 2, Hd, ATT_DIM), 1.0),
        'cache_dil16_kv': nrm((L, DEC_BATCH, wl[2], 2, Hd, ATT_DIM), 1.0),
        'w_in': nrm((L, D, N_IN_COLS), D ** -0.5),
        'lru_conv_w': nrm((L, CONV_W, LRU_WIDTH), CONV_W ** -0.5),
        'lru_conv_b': nrm((L, LRU_WIDTH), 0.01),
        'lru_wa': nrm((L, LRU_BLOCKS, LRU_BLOCK, LRU_BLOCK), LRU_BLOCK ** -0.5),
        'lru_ba': nrm((L, LRU_WIDTH), 0.01),
        'lru_wx': nrm((L, LRU_BLOCKS, LRU_BLOCK, LRU_BLOCK), LRU_BLOCK ** -0.5),
        'lru_bx': nrm((L, LRU_WIDTH), 0.01),
        'lru_lambda': lru_lambda,
        'ret_gn_g': 1.0 + nrm((L, RET_WIDTH), 0.02),
        'mlstm_conv_w': nrm((L, CONV_W, MLSTM_WIDTH), CONV_W ** -0.5),
        'mlstm_conv_b': nrm((L, MLSTM_WIDTH), 0.01),
        'mlstm_wq': nrm((L, Hm, dm, dm), dm ** -0.5),
        'mlstm_wk': nrm((L, Hm, dm, dm), dm ** -0.5),
        'mlstm_w_gates': nrm((L, 3 * MLSTM_WIDTH, 2 * Hm), (3 * MLSTM_WIDTH) ** -0.5),
        'mlstm_b_gates': mlstm_b_gates,
        'mlstm_skip': 1.0 + nrm((L, MLSTM_WIDTH), 0.02),
        'mlstm_gn_g': 1.0 + nrm((L, MLSTM_WIDTH), 0.02),
        'w_br_a': nrm((L, LRU_WIDTH, D), LRU_WIDTH ** -0.5),
        'w_br_b': nrm((L, RET_WIDTH, D), RET_WIDTH ** -0.5),
        'w_br_c': nrm((L, MLSTM_WIDTH, D), MLSTM_WIDTH ** -0.5),
        'w_br_d': nrm((L, DIL_OUT, D), DIL_OUT ** -0.5),
        'w_out': nrm((L, D, D), BETA * D ** -0.5),
        'ln1_g': 1.0 + nrm((L, D), 0.02),
        'ln1_b': nrm((L, D), 0.02),
        'w_up': nrm((L, D, D_FF), D ** -0.5),
        'b_up': nrm((L, D_FF), 0.01),
        'w_down': nrm((L, D_FF, D), BETA * D_FF ** -0.5),
        'b_down': nrm((L, D), 0.01),
        'ln2_g': 1.0 + nrm((L, D), 0.02),
        'ln2_b': nrm((L, D), 0.02),
    }


def reference(x_prompt, x_sample, state_lru_conv, state_lru_h, state_ret, state_mlstm_conv,
              state_mlstm_c, state_mlstm_n, state_mlstm_m, cache_dil1_kv, cache_dil4_kv, cache_dil16_kv,
              w_in, lru_conv_w, lru_conv_b, lru_wa, lru_ba, lru_wx, lru_bx, lru_lambda, ret_gn_g,
              mlstm_conv_w, mlstm_conv_b, mlstm_wq, mlstm_wk, mlstm_w_gates, mlstm_b_gates, mlstm_skip,
              mlstm_gn_g, w_br_a, w_br_b, w_br_c, w_br_d, w_out, ln1_g, ln1_b, w_up, b_up, w_down, b_down,
              ln2_g, ln2_b):
    pos_p = jnp.arange(x_prompt.shape[1], dtype=jnp.int32)
    pos_s = PAST_LEN + jnp.arange(x_sample.shape[1], dtype=jnp.int32)
    y_p, y_s = x_prompt, x_sample
    new_p, new_s = [], []
    for l in range(DEPTH):
        p = dict(w_in=w_in[l], lru_conv_w=lru_conv_w[l], lru_conv_b=lru_conv_b[l], lru_wa=lru_wa[l],
                 lru_ba=lru_ba[l], lru_wx=lru_wx[l], lru_bx=lru_bx[l], lru_lambda=lru_lambda[l],
                 ret_gn_g=ret_gn_g[l], mlstm_conv_w=mlstm_conv_w[l], mlstm_conv_b=mlstm_conv_b[l],
                 mlstm_wq=mlstm_wq[l], mlstm_wk=mlstm_wk[l], mlstm_w_gates=mlstm_w_gates[l],
                 mlstm_b_gates=mlstm_b_gates[l], mlstm_skip=mlstm_skip[l], mlstm_gn_g=mlstm_gn_g[l],
                 w_br_a=w_br_a[l], w_br_b=w_br_b[l], w_br_c=w_br_c[l], w_br_d=w_br_d[l], w_out=w_out[l],
                 ln1_g=ln1_g[l], ln1_b=ln1_b[l], w_up=w_up[l], b_up=b_up[l], w_down=w_down[l],
                 b_down=b_down[l], ln2_g=ln2_g[l], ln2_b=ln2_b[l])
        y_p, ns_p = _layer(y_p, p, _zero_state(x_prompt.shape[0], x_prompt.dtype), pos_p, None)
        new_p.append(ns_p)
        st = dict(lru_conv=state_lru_conv[l], lru_h=state_lru_h[l], ret=state_ret[l],
                  mlstm_conv=state_mlstm_conv[l], mlstm_c=state_mlstm_c[l], mlstm_n=state_mlstm_n[l],
                  mlstm_m=state_mlstm_m[l])
        y_s, ns_s = _layer(y_s, p, st, pos_s, (cache_dil1_kv[l], cache_dil4_kv[l], cache_dil16_kv[l]))
        new_s.append(ns_s)
    p_lru_conv, s_lru_conv = _stack(new_p, 'lru_conv'), _stack(new_s, 'lru_conv')
    p_lru_h, s_lru_h = _stack(new_p, 'lru_h'), _stack(new_s, 'lru_h')
    p_ret, s_ret = _stack(new_p, 'ret'), _stack(new_s, 'ret')
    p_mconv, s_mconv = _stack(new_p, 'mlstm_conv'), _stack(new_s, 'mlstm_conv')
    p_mc, s_mc = _stack(new_p, 'mlstm_c'), _stack(new_s, 'mlstm_c')
    p_mn, s_mn = _stack(new_p, 'mlstm_n'), _stack(new_s, 'mlstm_n')
    p_mm, s_mm = _stack(new_p, 'mlstm_m'), _stack(new_s, 'mlstm_m')
    p_kv1, s_kv1 = _stack(new_p, 'dil1_kv'), _stack(new_s, 'dil1_kv')
    p_kv4, s_kv4 = _stack(new_p, 'dil4_kv'), _stack(new_s, 'dil4_kv')
    p_kv16, s_kv16 = _stack(new_p, 'dil16_kv'), _stack(new_s, 'dil16_kv')
    return (y_p, y_s,
            p_lru_conv, p_lru_h, p_ret, p_mconv, p_mc, p_mn, p_mm, p_kv1, p_kv4, p_kv16,
            s_lru_conv, s_lru_h, s_ret, s_mconv, s_mc, s_mn, s_mm, s_kv1, s_kv4, s_kv16)
```

```python
import functools
import math

import jax
import jax.numpy as jnp
import numpy as np
from jax import lax
from jax.experimental import pallas as pl
from jax.experimental.pallas import tpu as pltpu

f32 = jnp.float32
bf16 = jnp.bfloat16

D_MODEL = 2048
DEPTH = 2
PAST_LEN = 2048
LRU_WIDTH = 512
LRU_C = 8.0
CONV_W = 4
RET_HEADS = 8
RET_DIM = 64
MLSTM_HEADS = 4
MLSTM_DIM = 128
DIL_GROUPS = ((128, 1), (512, 4), (2048, 16))
DIL_SPAN = 128
DIL_HEADS = 4
ATT_DIM = 64
ATT_SCALE = ATT_DIM ** -0.5
ROPE_THETA = 10000.0
D_FF = 4 * D_MODEL
CHUNK = 128
LN_EPS = 1e-5
GN_EPS = 1e-6
ALPHA = (2 * DEPTH) ** 0.25
N_MIX_COLS = 6912
N_GATE_COLS = 4 * D_MODEL
COL_A = 0
COL_B = 1024
COL_C = 3072
COL_DQ = 4608
COL_DK = 5376
COL_DV = 6144

TILE = 128
VMEM_LIMIT = 56 * 1024 * 1024


def _cparams(n_axes):
    return pltpu.CompilerParams(dimension_semantics=("arbitrary",) * n_axes, vmem_limit_bytes=VMEM_LIMIT)


def _dot(a, b):
    return jnp.dot(a.astype(bf16), b.astype(bf16), preferred_element_type=f32)


def _dot_nt(a, b):
    return lax.dot_general(a.astype(bf16), b.astype(bf16), (((1,), (1,)), ((), ())), preferred_element_type=f32)


def _dot_tn(a, b):
    return lax.dot_general(a.astype(bf16), b.astype(bf16), (((0,), (0,)), ((), ())), preferred_element_type=f32)


def _sigmoid(x):
    return 1.0 / (1.0 + jnp.exp(-x))


def _silu(x):
    return x * _sigmoid(x)


def _expm1(x):
    u = jnp.exp(x)
    safe = jnp.where((u == 1.0) | (x < -20.0), 0.5, u)
    return jnp.where(u == 1.0, x, jnp.where(x < -20.0, u - 1.0, (u - 1.0) * x / jnp.log(safe)))


def _gelu_tanh(x):
    return 0.5 * x * (1.0 + jnp.tanh(math.sqrt(2.0 / math.pi) * (x + 0.044715 * (x * x * x))))


def _layer_norm_rows(v, g, b):
    mu = jnp.mean(v, axis=-1, keepdims=True)
    c = v - mu
    var = jnp.mean(c * c, axis=-1, keepdims=True)
    return c * lax.rsqrt(var + LN_EPS) * g + b


def _rotate_heads(x, cos, sin_signed):
    lane = lax.broadcasted_iota(jnp.int32, (x.shape[0], 128), 1)
    low = (lane & 63) < 32
    outs = []
    for p in range(x.shape[1] // 128):
        xs = x[:, 128 * p:128 * (p + 1)]
        partner = jnp.where(low, pltpu.roll(xs, 96, 1), pltpu.roll(xs, 32, 1))
        outs.append(xs * cos + partner * sin_signed)
    return outs[0] if len(outs) == 1 else jnp.concatenate(outs, axis=1)


def _pad_rows(x, rows):
    if x.shape[0] == rows:
        return x
    return jnp.concatenate([x, jnp.zeros((rows - x.shape[0], x.shape[1]), x.dtype)], axis=0)


def _mm_kernel(x_ref, w_ref, o_ref):
    o_ref[...] = jnp.dot(x_ref[...], w_ref[...], preferred_element_type=f32)


def _project(x_bf, w_bf, layer, col0, ncols, tm, tn, name):
    m, k = x_bf.shape
    cb0 = col0 // tn
    assert col0 % tn == 0 and ncols % tn == 0 and m % tm == 0
    return pl.pallas_call(
        _mm_kernel,
        grid=(ncols // tn, m // tm),
        in_specs=[pl.BlockSpec((tm, k), lambda j, i: (i, 0)),
                  pl.BlockSpec((None, k, tn), lambda j, i: (layer, 0, cb0 + j))],
        out_specs=pl.BlockSpec((tm, tn), lambda j, i: (i, j)),
        out_shape=jax.ShapeDtypeStruct((m, ncols), f32),
        compiler_params=_cparams(2),
        name=name,
    )(x_bf, w_bf)


def _lru_kernel(x_ref, conv0_ref, h0_ref, cw_ref, cb_ref, wa_ref, ba_ref, wx_ref, bx_ref, lam_ref,
                y_ref, convo_ref, ho_ref, xp_ref, a_ref, b_ref, hc_ref, *, n, tc):
    c = pl.program_id(0)

    @pl.when(c == 0)
    def _():
        xp_ref[:, 5:8, :] = conv0_ref[...]
        hc_ref[...] = h0_ref[...]

    @pl.when(c > 0)
    def _():
        xp_ref[:, 5:8, :] = xp_ref[:, tc + 5:tc + 8, :]

    x = x_ref[:, :, 0:LRU_WIDTH]
    gate = x_ref[:, :, LRU_WIDTH:2 * LRU_WIDTH]
    xp_ref[:, 8:8 + tc, :] = x
    cw = cw_ref[...]
    ac = xp_ref[:, 5:5 + tc, :] * cw[0:1, :]
    ac = ac + xp_ref[:, 6:6 + tc, :] * cw[1:2, :]
    ac = ac + xp_ref[:, 7:7 + tc, :] * cw[2:3, :]
    ac = ac + x * cw[3:4, :]
    ac = ac + cb_ref[...]

    ac2 = ac.reshape(n * tc, LRU_WIDTH)
    ra, ri = [], []
    for p in range(4):
        seg = ac2[:, 128 * p:128 * (p + 1)].astype(bf16)
        ra.append(jnp.dot(seg, wa_ref[p], preferred_element_type=f32))
        ri.append(jnp.dot(seg, wx_ref[p], preferred_element_type=f32))
    r = _sigmoid(jnp.concatenate(ra, axis=1) + ba_ref[...])
    i = _sigmoid(jnp.concatenate(ri, axis=1) + bx_ref[...])
    nl = -lam_ref[...]
    softplus = jnp.maximum(nl, 0.0) + jnp.log1p(jnp.exp(-jnp.abs(nl)))
    log_a = -LRU_C * r * softplus
    a = jnp.exp(log_a)
    bt = jnp.sqrt(-_expm1(2.0 * log_a)) * (i * ac2)
    a_ref[...] = a.reshape(n, tc, LRU_WIDTH)
    b_ref[...] = bt.reshape(n, tc, LRU_WIDTH)

    def step(t, h):
        h = a_ref[:, pl.ds(t, 1), :] * h + b_ref[:, pl.ds(t, 1), :]
        b_ref[:, pl.ds(t, 1), :] = h
        return h

    h = lax.fori_loop(0, tc, step, hc_ref[...])
    hc_ref[...] = h
    y_ref[...] = b_ref[...] * _gelu_tanh(gate)

    @pl.when(c == pl.num_programs(0) - 1)
    def _():
        ho_ref[...] = h
        convo_ref[...] = xp_ref[:, tc + 5:tc + 8, :]


def _lru_branch(proj3, conv0, h0, conv0_map, h0_map, prm, layer, tc, name):
    n, t, _ = proj3.shape
    w2 = lambda c: (layer, 0, 0)
    w3 = lambda c: (layer, 0, 0, 0)
    kern = functools.partial(_lru_kernel, n=n, tc=tc)
    return pl.pallas_call(
        kern,
        grid=(t // tc,),
        in_specs=[pl.BlockSpec((n, tc, 2 * LRU_WIDTH), lambda c: (0, c, COL_A // (2 * LRU_WIDTH))),
                  conv0_map, h0_map,
                  pl.BlockSpec((None, CONV_W, LRU_WIDTH), w2),
                  pl.BlockSpec((None, 1, LRU_WIDTH), w2),
                  pl.BlockSpec((None, 4, 128, 128), w3),
                  pl.BlockSpec((None, 1, LRU_WIDTH), w2),
                  pl.BlockSpec((None, 4, 128, 128), w3),
                  pl.BlockSpec((None, 1, LRU_WIDTH), w2),
                  pl.BlockSpec((None, 1, LRU_WIDTH), w2)],
        out_specs=[pl.BlockSpec((n, tc, LRU_WIDTH), lambda c: (0, c, 0)),
                   pl.BlockSpec((n, CONV_W - 1, LRU_WIDTH), lambda c: (0, 0, 0)),
                   pl.BlockSpec((n, 1, LRU_WIDTH), lambda c: (0, 0, 0))],
        out_shape=[jax.ShapeDtypeStruct((n, t, LRU_WIDTH), f32),
                   jax.ShapeDtypeStruct((n, CONV_W - 1, LRU_WIDTH), f32),
                   jax.ShapeDtypeStruct((n, 1, LRU_WIDTH), f32)],
        scratch_shapes=[pltpu.VMEM((n, tc + 8, LRU_WIDTH), f32),
                        pltpu.VMEM((n, tc, LRU_WIDTH), f32),
                        pltpu.VMEM((n, tc, LRU_WIDTH), f32),
                        pltpu.VMEM((n, 1, LRU_WIDTH), f32)],
        compiler_params=_cparams(1),
        name=name,
    )(proj3, conv0, h0, prm["lru_conv_w"], prm["lru_conv_b"], prm["lru_wa2"], prm["lru_ba"],
      prm["lru_wx2"], prm["lru_bx"], prm["lru_lambda"])


def _ret_kernel(qk_ref, vg_ref, cos_ref, sin_ref, s0_ref, gn_ref, y_ref, so_ref, s_ref, *, lv):
    c = pl.program_id(1)

    @pl.when(c == 0)
    def _():
        s_ref[...] = s0_ref[...]

    cos = cos_ref[...]
    sin = sin_ref[...]
    qk = _rotate_heads(qk_ref[...], cos, sin)
    q = _pad_rows(qk[:, 0:512], TILE)
    k = _pad_rows(qk[:, 512:1024] * (RET_DIM ** -0.5), TILE)
    v = _pad_rows(vg_ref[:, 0:512], TILE)
    gate = vg_ref[:, 512:1024]
    li = lax.broadcasted_iota(jnp.int32, (TILE, TILE), 0)
    mi = lax.broadcasted_iota(jnp.int32, (TILE, TILE), 1)
    rel = (li - mi).astype(f32)
    lcol = lax.broadcasted_iota(jnp.int32, (TILE, 1), 0).astype(f32)
    outs = []
    for h in range(RET_HEADS):
        log_g = math.log1p(-(2.0 ** (-5.0 - h)))
        d_in = jnp.where(rel >= 0, jnp.exp(jnp.maximum(rel, 0.0) * log_g), 0.0)
        d_q = jnp.exp((lcol + 1.0) * log_g)
        d_k = jnp.exp((lv - 1.0 - lcol) * log_g)
        d_s = math.exp(lv * log_g)
        sl = slice(RET_DIM * h, RET_DIM * (h + 1))
        qh, kh, vh = q[:, sl], k[:, sl], v[:, sl]
        s_h = s_ref[h]
        sc = _dot_nt(qh, kh) * d_in
        o = _dot(sc, vh) + _dot(qh, s_h) * d_q
        s_ref[h] = s_h * d_s + _dot_tn(kh * d_k, vh)
        mu = jnp.mean(o, axis=-1, keepdims=True)
        oc = o - mu
        var = jnp.mean(oc * oc, axis=-1, keepdims=True)
        outs.append(oc * lax.rsqrt(var + GN_EPS))
    normed = jnp.concatenate(outs, axis=1)[0:gate.shape[0], :]
    y_ref[...] = normed * gn_ref[...] * _silu(gate)

    @pl.when(c == pl.num_programs(1) - 1)
    def _():
        so_ref[...] = s_ref[...]


def _ret_branch(proj2, cos, sin, s0, s0_spec, prm, layer, n, t, lin, name):
    nchunk = t // lin
    kern = functools.partial(_ret_kernel, lv=float(lin))
    return pl.pallas_call(
        kern,
        grid=(n, nchunk),
        in_specs=[pl.BlockSpec((lin, 1024), lambda b, c: (b * nchunk + c, COL_B // 1024)),
                  pl.BlockSpec((lin, 1024), lambda b, c: (b * nchunk + c, COL_B // 1024 + 1)),
                  pl.BlockSpec((lin, 128), lambda b, c: (c, 0)),
                  pl.BlockSpec((lin, 128), lambda b, c: (c, 0)),
                  s0_spec,
                  pl.BlockSpec((None, 1, 512), lambda b, c: (layer, 0, 0))],
        out_specs=[pl.BlockSpec((lin, 512), lambda b, c: (b * nchunk + c, 0)),
                   pl.BlockSpec((None, RET_HEADS, RET_DIM, RET_DIM), lambda b, c: (b, 0, 0, 0))],
        out_shape=[jax.ShapeDtypeStruct((n * t, 512), f32),
                   jax.ShapeDtypeStruct((n, RET_HEADS, RET_DIM, RET_DIM), f32)],
        scratch_shapes=[pltpu.VMEM((RET_HEADS, RET_DIM, RET_DIM), f32)],
        compiler_params=_cparams(2),
        name=name,
    )(proj2, proj2, cos, sin, s0, prm["ret_gn_g"])


def _split3(x):
    hi = x.astype(bf16)
    r1 = x - hi.astype(f32)
    mid = r1.astype(bf16)
    lo = (r1 - mid.astype(f32)).astype(bf16)
    return hi, mid, lo


def _mlstm_kernel(qk_ref, v_ref, z_ref, conv0_ref, c0_ref, n0_ref, m0_ref, cw_ref, cb_ref, wq_ref, wk_ref,
                  wg_ref, bg_ref, skip_ref, gn_ref,
                  y_ref, convo_ref, co_ref, no_ref, mo_ref,
                  xp_ref, cs_ref, ns_ref, ms_ref, *, lin):
    c = pl.program_id(1)
    lv = lin

    @pl.when(c == 0)
    def _():
        xp_ref[5:8, :] = conv0_ref[...]
        cs_ref[...] = c0_ref[...]
        ns_ref[...] = n0_ref[...]
        ms_ref[...] = m0_ref[...]

    @pl.when(c > 0)
    def _():
        xp_ref[5:8, :] = xp_ref[lin + 5:lin + 8, :]

    x = qk_ref[...]
    xp_ref[8:8 + lin, :] = x
    cw = cw_ref[...]
    cc = xp_ref[5:5 + lin, :] * cw[0:1, :]
    cc = cc + xp_ref[6:6 + lin, :] * cw[1:2, :]
    cc = cc + xp_ref[7:7 + lin, :] * cw[2:3, :]
    cc = cc + x * cw[3:4, :]
    cc = cc + cb_ref[...]
    c_act = _silu(cc)
    vin = v_ref[...]
    cq, ck = [], []
    for h in range(MLSTM_HEADS):
        seg = c_act[:, 128 * h:128 * (h + 1)].astype(bf16)
        cq.append(jnp.dot(seg, wq_ref[h], preferred_element_type=f32))
        ck.append(jnp.dot(seg, wk_ref[h], preferred_element_type=f32))
    cq = jnp.concatenate(cq, axis=1)
    ck = jnp.concatenate(ck, axis=1)
    gpre = (jnp.dot(cq.astype(bf16), wg_ref[0:512, :], preferred_element_type=f32)
            + jnp.dot(ck.astype(bf16), wg_ref[512:1024, :], preferred_element_type=f32)
            + jnp.dot(vin.astype(bf16), wg_ref[1024:1536, :], preferred_element_type=f32)) + bg_ref[...]

    g = _pad_rows(gpre, TILE)
    q = _pad_rows(cq, TILE)
    k = _pad_rows(ck, TILE) * (MLSTM_DIM ** -0.5)
    v = _pad_rows(vin, TILE)
    lf = jnp.minimum(g, 0.0) - jnp.log1p(jnp.exp(-jnp.abs(g)))
    li = lax.broadcasted_iota(jnp.int32, (TILE, TILE), 0)
    mi = lax.broadcasted_iota(jnp.int32, (TILE, TILE), 1)
    causal = li >= mi
    tri = jnp.where(causal, 1.0, 0.0).astype(bf16)
    hi, mid, lo = _split3(lf)
    fc = (jnp.dot(tri, hi, preferred_element_type=f32) + jnp.dot(tri, mid, preferred_element_type=f32)
          + jnp.dot(tri, lo, preferred_element_type=f32))
    fct = fc.T
    gt = g.T
    rowi = lax.broadcasted_iota(jnp.int32, (TILE, 1), 0)
    valid = rowi < lv
    lane = lax.broadcasted_iota(jnp.int32, (1, 128), 1)
    m_all = ms_ref[...]
    m_next = m_all
    outs = []
    for h in range(MLSTM_HEADS):
        sl = slice(MLSTM_DIM * h, MLSTM_DIM * (h + 1))
        qh, kh, vh = q[:, sl], k[:, sl], v[:, sl]
        f_col = fc[:, 4 + h:5 + h]
        f_row = fct[4 + h:5 + h, :]
        ig_row = gt[h:h + 1, :]
        ig_col = g[:, h:h + 1]
        m_prev = m_all[:, h:h + 1]
        log_d = jnp.where(causal, f_col - f_row + ig_row, -jnp.inf)
        log_inter = f_col + m_prev
        m_t = jnp.maximum(jnp.max(log_d, axis=1, keepdims=True), log_inter)
        dm = jnp.exp(log_d - m_t)
        inter = jnp.exp(log_inter - m_t)
        c_h = cs_ref[h]
        n_h = ns_ref[h:h + 1, :]
        s = _dot_nt(qh, kh) * dm
        num = _dot(s, vh) + inter * _dot(qh, c_h)
        den = jnp.sum(s, axis=1, keepdims=True) + inter * jnp.sum(qh * n_h, axis=1, keepdims=True)
        hout = num / jnp.maximum(jnp.abs(den), jnp.exp(-m_t))
        m_new = m_t[lv - 1:lv, :]
        f_last = f_col[lv - 1:lv, :]
        wk = jnp.where(valid, jnp.exp(f_last - f_col + ig_col - m_new), 0.0)
        dec = jnp.exp(f_last + m_prev - m_new)
        kw = kh * wk
        cs_ref[h] = dec * c_h + _dot_tn(kw, vh)
        ns_ref[h:h + 1, :] = dec * n_h + jnp.sum(kw, axis=0, keepdims=True)
        m_next = jnp.where(lane == h, m_new, m_next)
        mu = jnp.mean(hout, axis=-1, keepdims=True)
        hc = hout - mu
        var = jnp.mean(hc * hc, axis=-1, keepdims=True)
        outs.append(hc * lax.rsqrt(var + GN_EPS))
    ms_ref[...] = m_next
    normed = jnp.concatenate(outs, axis=1)[0:lin, :]
    y_ref[...] = _sigmoid(z_ref[...]) * (normed * gn_ref[...] + skip_ref[...] * c_act)

    @pl.when(c == pl.num_programs(1) - 1)
    def _():
        convo_ref[...] = xp_ref[lin + 5:lin + 8, :]
        co_ref[...] = cs_ref[...]
        no_ref[...] = ns_ref[...]
        mo_ref[...] = ms_ref[...]


def _mlstm_branch(proj2, conv0, c0, n0, m0, state_specs, prm, layer, n, t, lin, name):
    nchunk = t // lin
    kern = functools.partial(_mlstm_kernel, lin=lin)
    w2 = lambda b, c: (layer, 0, 0)
    w3 = lambda b, c: (layer, 0, 0, 0)
    cb = COL_C // 512
    return pl.pallas_call(
        kern,
        grid=(n, nchunk),
        in_specs=[pl.BlockSpec((lin, 512), lambda b, c: (b * nchunk + c, cb)),
                  pl.BlockSpec((lin, 512), lambda b, c: (b * nchunk + c, cb + 1)),
                  pl.BlockSpec((lin, 512), lambda b, c: (b * nchunk + c, cb + 2)),
                  *state_specs,
                  pl.BlockSpec((None, CONV_W, 512), w2),
                  pl.BlockSpec((None, 1, 512), w2),
                  pl.BlockSpec((None, 4, 128, 128), w3),
                  pl.BlockSpec((None, 4, 128, 128), w3),
                  pl.BlockSpec((None, 1536, 128), w2),
                  pl.BlockSpec((None, 1, 128), w2),
                  pl.BlockSpec((None, 1, 512), w2),
                  pl.BlockSpec((None, 1, 512), w2)],
        out_specs=[pl.BlockSpec((lin, 512), lambda b, c: (b * nchunk + c, 0)),
                   pl.BlockSpec((None, CONV_W - 1, 512), lambda b, c: (b, 0, 0)),
                   pl.BlockSpec((None, 4, 128, 128), lambda b, c: (b, 0, 0, 0)),
                   pl.BlockSpec((None, 4, 128), lambda b, c: (b, 0, 0)),
                   pl.BlockSpec((None, 1, 128), lambda b, c: (b, 0, 0))],
        out_shape=[jax.ShapeDtypeStruct((n * t, 512), f32),
                   jax.ShapeDtypeStruct((n, CONV_W - 1, 512), f32),
                   jax.ShapeDtypeStruct((n, 4, 128, 128), f32),
                   jax.ShapeDtypeStruct((n, 4, 128), f32),
                   jax.ShapeDtypeStruct((n, 1, 128), f32)],
        scratch_shapes=[pltpu.VMEM((lin + 8, 512), f32),
                        pltpu.VMEM((4, 128, 128), f32),
                        pltpu.VMEM((4, 128), f32),
                        pltpu.VMEM((1, 128), f32)],
        compiler_params=_cparams(2),
        name=name,
    )(proj2, proj2, proj2, conv0, c0, n0, m0, prm["mlstm_conv_w"], prm["mlstm_conv_b"], prm["mlstm_wq"],
      prm["mlstm_wk"], prm["mlstm_wg"], prm["mlstm_bg"], prm["mlstm_skip"], prm["mlstm_gn_g"])


def _dil_prompt_kernel(q_ref, k_ref, v_ref, cos_ref, sin_ref, o_ref, l_ref, kv_ref, qs_ref, kp_ref, vp_ref, *, m):
    cos = cos_ref[...]
    sin = sin_ref[...]
    k = _rotate_heads(k_ref[...], cos, sin)
    v = v_ref[...]
    qs_ref[...] = _rotate_heads(q_ref[...], cos, sin)
    zeros = jnp.zeros((DIL_SPAN, 256), f32)
    kp_ref[0:DIL_SPAN, :] = zeros
    vp_ref[0:DIL_SPAN, :] = zeros
    kp_ref[DIL_SPAN:DIL_SPAN + m, :] = k
    vp_ref[DIL_SPAN:DIL_SPAN + m, :] = v
    kv_ref[:, 0:256] = k[m - DIL_SPAN:m, :]
    kv_ref[:, 256:512] = v[m - DIL_SPAN:m, :]
    qi = lax.broadcasted_iota(jnp.int32, (DIL_SPAN, 2 * DIL_SPAN), 0) + DIL_SPAN
    kj = lax.broadcasted_iota(jnp.int32, (DIL_SPAN, 2 * DIL_SPAN), 1)
    rel = qi - kj
    band = (rel >= 0) & (rel <= DIL_SPAN)

    def block(b, carry):
        r0 = pl.multiple_of(b * DIL_SPAN, DIL_SPAN)
        qb = qs_ref[pl.ds(r0, DIL_SPAN), :]
        kw = kp_ref[pl.ds(r0, 2 * DIL_SPAN), :]
        vw = vp_ref[pl.ds(r0, 2 * DIL_SPAN), :]
        mask = band & ((kj + b * (2 * DIL_SPAN)) >= DIL_SPAN)
        outs, lses = [], []
        for h in range(DIL_HEADS):
            sl = slice(ATT_DIM * h, ATT_DIM * (h + 1))
            s = _dot_nt(qb[:, sl], kw[:, sl]) * ATT_SCALE
            s = jnp.where(mask, s, -jnp.inf)
            mx = jnp.max(s, axis=1, keepdims=True)
            lse = mx + jnp.log(jnp.sum(jnp.exp(s - mx), axis=1, keepdims=True))
            p = jnp.exp(s - lse)
            outs.append(_dot(p, vw[:, sl]))
            lses.append(jnp.broadcast_to(lse, (DIL_SPAN, ATT_DIM)))
        o_ref[pl.ds(r0, DIL_SPAN), :] = jnp.concatenate(outs, axis=1)
        l_ref[pl.ds(r0, DIL_SPAN), :] = jnp.concatenate(lses, axis=1)
        return carry

    lax.fori_loop(0, m // DIL_SPAN, block, 0)


def _dil_prompt_group(proj2, cos, sin, n, t, g, name):
    win, dil = DIL_GROUPS[g]
    m = t // dil
    rows = proj2.shape[0] // dil
    pv = proj2.reshape(rows, dil * N_MIX_COLS)
    cosv = cos.reshape(m, dil * 128)
    sinv = sin.reshape(m, dil * 128)
    cpr = N_MIX_COLS // 256
    kern = functools.partial(_dil_prompt_kernel, m=m)
    o, l, kv = pl.pallas_call(
        kern,
        grid=(n, dil),
        in_specs=[pl.BlockSpec((m, 256), lambda b, r: (b, r * cpr + COL_DQ // 256 + g)),
                  pl.BlockSpec((m, 256), lambda b, r: (b, r * cpr + COL_DK // 256 + g)),
                  pl.BlockSpec((m, 256), lambda b, r: (b, r * cpr + COL_DV // 256 + g)),
                  pl.BlockSpec((m, 128), lambda b, r: (0, r)),
                  pl.BlockSpec((m, 128), lambda b, r: (0, r))],
        out_specs=[pl.BlockSpec((m, 256), lambda b, r: (b, r)),
                   pl.BlockSpec((m, 256), lambda b, r: (b, r)),
                   pl.BlockSpec((DIL_SPAN, 512), lambda b, r: (b, r))],
        out_shape=[jax.ShapeDtypeStruct((n * m, dil * 256), f32),
                   jax.ShapeDtypeStruct((n * m, dil * 256), f32),
                   jax.ShapeDtypeStruct((n * DIL_SPAN, dil * 512), f32)],
        scratch_shapes=[pltpu.VMEM((m, 256), f32),
                        pltpu.VMEM((m + DIL_SPAN, 256), f32),
                        pltpu.VMEM((m + DIL_SPAN, 256), f32)],
        compiler_params=_cparams(2),
        name=name,
    )(pv, pv, pv, cosv, sinv)
    return o.reshape(n * t, 256), l.reshape(n * t, 256), kv.reshape(n, min(win, t), 2, DIL_HEADS, ATT_DIM)


def _dil_sample_kernel(q_ref, k_ref, v_ref, cos_ref, sin_ref, c1_ref, c4_ref, c16_ref,
                       o1_ref, o4_ref, o16_ref, l1_ref, l4_ref, l16_ref, kv1_ref, kv4_ref, kv16_ref, *, s_new):
    cos = cos_ref[...]
    sin = sin_ref[...]
    nq = DIL_HEADS * s_new
    lane_head = lax.broadcasted_iota(jnp.int32, (s_new, 256), 1) >> 6
    qrow = lax.broadcasted_iota(jnp.int32, (nq, 1), 0)
    qpos = PAST_LEN + (qrow & (s_new - 1))
    caches = (c1_ref, c4_ref, c16_ref)
    o_refs = (o1_ref, o4_ref, o16_ref)
    l_refs = (l1_ref, l4_ref, l16_ref)
    kv_refs = (kv1_ref, kv4_ref, kv16_ref)
    for g, (win, dil) in enumerate(DIL_GROUPS):
        sl = slice(256 * g, 256 * (g + 1))
        qg = _rotate_heads(q_ref[:, sl], cos, sin)
        kg = _rotate_heads(k_ref[:, sl], cos, sin)
        vg = v_ref[:, sl]
        kv_refs[g][:, 0:256] = kg
        kv_refs[g][:, 256:512] = vg
        qbd = jnp.concatenate([jnp.where(lane_head == h, qg, 0.0) for h in range(DIL_HEADS)], axis=0)
        lb = min(win, PAST_LEN)
        if g < 2:
            blk = caches[g][...]
            ncol = lb
            col = lax.broadcasted_iota(jnp.int32, (1, ncol), 1)
            kpos = PAST_LEN - lb + col
        else:
            blk = caches[g][...].reshape(DIL_SPAN * s_new, 512)
            ncol = DIL_SPAN * s_new
            col = lax.broadcasted_iota(jnp.int32, (1, ncol), 1)
            kpos = PAST_LEN - lb + dil * (col >> 3) + (col & (s_new - 1))
        kc = blk[:, 0:256]
        vc = blk[:, 256:512]
        dist = qpos - kpos
        ok_c = (dist >= 0) & (dist <= DIL_SPAN * dil) & ((dist & (dil - 1)) == 0)
        ncol_n = 128
        coln = lax.broadcasted_iota(jnp.int32, (1, ncol_n), 1)
        dist_n = qpos - (PAST_LEN + coln)
        ok_n = (coln < s_new) & (dist_n >= 0) & (dist_n <= DIL_SPAN * dil) & ((dist_n & (dil - 1)) == 0)
        kn = _pad_rows(kg, ncol_n)
        vn = _pad_rows(vg, ncol_n)
        sc = jnp.where(ok_c, _dot_nt(qbd, kc) * ATT_SCALE, -jnp.inf)
        sn = jnp.where(ok_n, _dot_nt(qbd, kn) * ATT_SCALE, -jnp.inf)
        mx = jnp.maximum(jnp.max(sc, axis=1, keepdims=True), jnp.max(sn, axis=1, keepdims=True))
        tot = jnp.sum(jnp.exp(sc - mx), axis=1, keepdims=True) + jnp.sum(jnp.exp(sn - mx), axis=1, keepdims=True)
        lse = mx + jnp.log(tot)
        o = _dot(jnp.exp(sc - lse), vc) + _dot(jnp.exp(sn - lse), vn)
        lse_b = jnp.broadcast_to(lse, (nq, 256))
        og = jnp.zeros((s_new, 256), f32)
        lg = jnp.zeros((s_new, 256), f32)
        for h in range(DIL_HEADS):
            og = og + jnp.where(lane_head == h, o[s_new * h:s_new * (h + 1), :], 0.0)
            lg = lg + jnp.where(lane_head == h, lse_b[s_new * h:s_new * (h + 1), :], 0.0)
        o_refs[g][...] = og
        l_refs[g][...] = lg


def _dil_sample(proj2, cos, sin, c1, c4, c16, layer, n, s_new, name):
    assert PAST_LEN >= DIL_GROUPS[-1][0] and s_new & (s_new - 1) == 0 and s_new <= 8
    c16v = c16.reshape(DEPTH, n, DIL_SPAN, 16, 512)
    c1v = c1.reshape(DEPTH, n, 128, 512)
    c4v = c4.reshape(DEPTH, n, 512, 512)
    kern = functools.partial(_dil_sample_kernel, s_new=s_new)
    row = lambda b: (b, 0)
    outs = pl.pallas_call(
        kern,
        grid=(n,),
        in_specs=[pl.BlockSpec((s_new, 768), lambda b: (b, COL_DQ // 768)),
                  pl.BlockSpec((s_new, 768), lambda b: (b, COL_DK // 768)),
                  pl.BlockSpec((s_new, 768), lambda b: (b, COL_DV // 768)),
                  pl.BlockSpec((s_new, 128), lambda b: (0, 0)),
                  pl.BlockSpec((s_new, 128), lambda b: (0, 0)),
                  pl.BlockSpec((None, None, 128, 512), lambda b: (layer, b, 0, 0)),
                  pl.BlockSpec((None, None, 512, 512), lambda b: (layer, b, 0, 0)),
                  pl.BlockSpec((None, None, DIL_SPAN, s_new, 512), lambda b: (layer, b, 0, 0, 0))],
        out_specs=[pl.BlockSpec((s_new, 256), row)] * 6 + [pl.BlockSpec((s_new, 512), row)] * 3,
        out_shape=[jax.ShapeDtypeStruct((n * s_new, 256), f32)] * 6
        + [jax.ShapeDtypeStruct((n * s_new, 512), f32)] * 3,
        compiler_params=_cparams(1),
        name=name,
    )(proj2, proj2, proj2, cos, sin, c1v, c4v, c16v)
    o = outs[0:3]
    l = outs[3:6]
    kv = [x.reshape(n, s_new, 2, DIL_HEADS, ATT_DIM) for x in outs[6:9]]
    return o, l, kv


def _merge_kernel(ga_ref, gb_ref, gc_ref, gd_ref, ya_ref, yb_ref, yc_ref, o1_ref, o4_ref, o16_ref,
                  l1_ref, l4_ref, l16_ref, wa_ref, wb_ref, wc_ref, wd_ref, wo_ref, u_ref, g_ref, b_ref,
                  x_ref, xbf_ref, acc_ref, y_s):
    j = pl.program_id(1)

    @pl.when(j == 0)
    def _():
        y_s[:, 0:512] = ya_ref[...].astype(bf16)
        y_s[:, 512:1024] = yb_ref[...].astype(bf16)
        y_s[:, 1024:1536] = yc_ref[...].astype(bf16)
        l1, l4, l16 = l1_ref[...], l4_ref[...], l16_ref[...]
        mx = jnp.maximum(jnp.maximum(l1, l4), l16)
        e1, e4, e16 = jnp.exp(l1 - mx), jnp.exp(l4 - mx), jnp.exp(l16 - mx)
        tot = e1 + e4 + e16
        yd = (e1 / tot) * o1_ref[...] + (e4 / tot) * o4_ref[...] + (e16 / tot) * o16_ref[...]
        y_s[:, 1536:1792] = yd.astype(bf16)

    merged = _sigmoid(ga_ref[...]) * jnp.dot(y_s[:, 0:512], wa_ref[...], preferred_element_type=f32)
    merged = merged + _sigmoid(gb_ref[...]) * jnp.dot(y_s[:, 512:1024], wb_ref[...], preferred_element_type=f32)
    merged = merged + _sigmoid(gc_ref[...]) * jnp.dot(y_s[:, 1024:1536], wc_ref[...], preferred_element_type=f32)
    merged = merged + _sigmoid(gd_ref[...]) * jnp.dot(y_s[:, 1536:1792], wd_ref[...], preferred_element_type=f32)
    contrib = jnp.dot(merged.astype(bf16), wo_ref[...], preferred_element_type=f32)

    @pl.when(j == 0)
    def _():
        acc_ref[...] = contrib

    @pl.when(j > 0)
    def _():
        acc_ref[...] += contrib

    @pl.when(j == pl.num_programs(1) - 1)
    def _():
        xn = _layer_norm_rows(ALPHA * u_ref[...] + acc_ref[...], g_ref[...], b_ref[...])
        x_ref[...] = xn
        xbf_ref[...] = xn.astype(bf16)


def _merge(gates, ys, d_o, d_l, u, prm, layer, tm, name):
    m = u.shape[0]
    tn = 512
    nj = D_MODEL // tn
    gspec = lambda br: pl.BlockSpec((tm, tn), lambda i, j: (i, br * nj + j))
    row512 = pl.BlockSpec((tm, 512), lambda i, j: (i, 0))
    row256 = pl.BlockSpec((tm, 256), lambda i, j: (i, 0))
    wbr = lambda kdim: pl.BlockSpec((None, kdim, tn), lambda i, j: (layer, 0, j))
    vec = pl.BlockSpec((None, 1, D_MODEL), lambda i, j: (layer, 0, 0))
    return pl.pallas_call(
        _merge_kernel,
        grid=(m // tm, nj),
        in_specs=[gspec(0), gspec(1), gspec(2), gspec(3), row512, row512, row512,
                  row256, row256, row256, row256, row256, row256,
                  wbr(512), wbr(512), wbr(512), wbr(256),
                  pl.BlockSpec((None, tn, D_MODEL), lambda i, j: (layer, j, 0)),
                  pl.BlockSpec((tm, D_MODEL), lambda i, j: (i, 0)), vec, vec],
        out_specs=[pl.BlockSpec((tm, D_MODEL), lambda i, j: (i, 0)),
                   pl.BlockSpec((tm, D_MODEL), lambda i, j: (i, 0))],
        out_shape=[jax.ShapeDtypeStruct((m, D_MODEL), f32), jax.ShapeDtypeStruct((m, D_MODEL), bf16)],
        scratch_shapes=[pltpu.VMEM((tm, D_MODEL), f32), pltpu.VMEM((tm, 1792), bf16)],
        compiler_params=_cparams(2),
        name=name,
    )(gates, gates, gates, gates, *ys, *d_o, *d_l, prm["w_br_a"], prm["w_br_b"], prm["w_br_c"], prm["w_br_d"],
      prm["w_out"], u, prm["ln1_g"], prm["ln1_b"])


def _ffn_kernel(xbf_ref, x_ref, wu_ref, bu_ref, wd_ref, bd_ref, g_ref, b_ref, o_ref, obf_ref, acc_ref):
    j = pl.program_id(1)
    hid = jnp.dot(xbf_ref[...], wu_ref[...], preferred_element_type=f32) + bu_ref[...]
    hid = jnp.square(jnp.maximum(hid, 0.0))
    contrib = jnp.dot(hid.astype(bf16), wd_ref[...], preferred_element_type=f32)

    @pl.when(j == 0)
    def _():
        acc_ref[...] = contrib

    @pl.when(j > 0)
    def _():
        acc_ref[...] += contrib

    @pl.when(j == pl.num_programs(1) - 1)
    def _():
        xn = _layer_norm_rows(ALPHA * x_ref[...] + acc_ref[...] + bd_ref[...], g_ref[...], b_ref[...])
        o_ref[...] = xn
        obf_ref[...] = xn.astype(bf16)


def _ffn(x, x_bf, prm, layer, tm, tc, name):
    m = x.shape[0]
    vec = pl.BlockSpec((None, 1, D_MODEL), lambda i, j: (layer, 0, 0))
    rows = pl.BlockSpec((tm, D_MODEL), lambda i, j: (i, 0))
    return pl.pallas_call(
        _ffn_kernel,
        grid=(m // tm, D_FF // tc),
        in_specs=[rows, rows,
                  pl.BlockSpec((None, D_MODEL, tc), lambda i, j: (layer, 0, j)),
                  pl.BlockSpec((None, 1, tc), lambda i, j: (layer, 0, j)),
                  pl.BlockSpec((None, tc, D_MODEL), lambda i, j: (layer, j, 0)),
                  vec, vec, vec],
        out_specs=[rows, rows],
        out_shape=[jax.ShapeDtypeStruct((m, D_MODEL), f32), jax.ShapeDtypeStruct((m, D_MODEL), bf16)],
        scratch_shapes=[pltpu.VMEM((tm, D_MODEL), f32)],
        compiler_params=_cparams(2),
        name=name,
    )(x_bf, x, prm["w_up"], prm["b_up"], prm["w_down"], prm["b_down"], prm["ln2_g"], prm["ln2_b"])


def _rope_tables(pos):
    half = ATT_DIM // 2
    inv = ROPE_THETA ** (-jnp.arange(half, dtype=f32) / half)
    ang = pos.astype(f32)[:, None] * inv[None, :]
    cos, sin = jnp.cos(ang), jnp.sin(ang)
    return jnp.concatenate([cos, cos, cos, cos], axis=1), jnp.concatenate([-sin, sin, -sin, sin], axis=1)


def _pair_blocks(w):
    l = w.shape[0]
    w = w.reshape(l, 4, 2, 64, 64)
    z = jnp.zeros((l, 4, 64, 64), w.dtype)
    top = jnp.concatenate([w[:, :, 0], z], axis=-1)
    bot = jnp.concatenate([z, w[:, :, 1]], axis=-1)
    return jnp.concatenate([top, bot], axis=-2).astype(bf16)


def _stream(x, prm, layer, n, t, lin, tc_lru, states, dil_caches, cos, sin, tag):
    xf, xb = x
    m = n * t
    tm = 1024
    proj = _project(xb, prm["w_in_mix"], layer, 0, N_MIX_COLS, tm, 1152, f"proj_mix_{tag}")
    gates = _project(xb, prm["w_in_gate"], layer, 0, N_GATE_COLS, tm, 1024, f"proj_gate_{tag}")

    if states is None:
        zc = jnp.zeros((n, CONV_W - 1, 512), f32)
        conv_a = (zc, pl.BlockSpec((n, CONV_W - 1, 512), lambda c: (0, 0, 0)))
        h_a = (jnp.zeros((n, 1, 512), f32), pl.BlockSpec((n, 1, 512), lambda c: (0, 0, 0)))
        s0 = (jnp.zeros((n, RET_HEADS, RET_DIM, RET_DIM), f32),
              pl.BlockSpec((None, RET_HEADS, RET_DIM, RET_DIM), lambda b, c: (b, 0, 0, 0)))
        mst = (zc, jnp.zeros((n, 4, 128, 128), f32), jnp.zeros((n, 4, 128), f32), jnp.zeros((n, 1, 128), f32))
        mspecs = [pl.BlockSpec((None, CONV_W - 1, 512), lambda b, c: (b, 0, 0)),
                  pl.BlockSpec((None, 4, 128, 128), lambda b, c: (b, 0, 0, 0)),
                  pl.BlockSpec((None, 4, 128), lambda b, c: (b, 0, 0)),
                  pl.BlockSpec((None, 1, 128), lambda b, c: (b, 0, 0))]
    else:
        conv_a = (states["lru_conv"], pl.BlockSpec((None, n, CONV_W - 1, 512), lambda c: (layer, 0, 0, 0)))
        h_a = (states["lru_h"], pl.BlockSpec((None, n, 1, 512), lambda c: (layer, 0, 0, 0)))
        s0 = (states["ret"], pl.BlockSpec((None, None, RET_HEADS, RET_DIM, RET_DIM),
                                          lambda b, c: (layer, b, 0, 0, 0)))
        mst = (states["mlstm_conv"], states["mlstm_c"], states["mlstm_n"], states["mlstm_m"])
        mspecs = [pl.BlockSpec((None, None, CONV_W - 1, 512), lambda b, c: (layer, b, 0, 0)),
                  pl.BlockSpec((None, None, 4, 128, 128), lambda b, c: (layer, b, 0, 0, 0)),
                  pl.BlockSpec((None, None, 4, 128), lambda b, c: (layer, b, 0, 0)),
                  pl.BlockSpec((None, None, 1, 128), lambda b, c: (layer, b, 0, 0))]

    y_a, lru_conv, lru_h = _lru_branch(proj.reshape(n, t, N_MIX_COLS), conv_a[0], h_a[0], conv_a[1], h_a[1],
                                       prm, layer, tc_lru, f"lru_{tag}")
    y_b, ret_s = _ret_branch(proj, cos, sin, s0[0], s0[1], prm, layer, n, t, lin, f"ret_{tag}")
    y_c, m_conv, m_c, m_n, m_m = _mlstm_branch(proj, *mst, mspecs, prm, layer, n, t, lin, f"mlstm_{tag}")

    if dil_caches is None:
        d_o, d_l, kvs = [], [], []
        for g in range(len(DIL_GROUPS)):
            o, l, kv = _dil_prompt_group(proj, cos, sin, n, t, g, f"dil{g}_{tag}")
            d_o.append(o)
            d_l.append(l)
            kvs.append(kv)
    else:
        d_o, d_l, kvs = _dil_sample(proj, cos, sin, *dil_caches, layer, n, t, f"dil_{tag}")

    x1, x1b = _merge(gates, (y_a.reshape(m, 512), y_b, y_c), d_o, d_l, xf, prm, layer, 256, f"merge_{tag}")
    x2 = _ffn(x1, x1b, prm, layer, 512, 512, f"ffn_{tag}")
    new = dict(lru_conv=lru_conv, lru_h=lru_h[:, 0, :], ret=ret_s, mlstm_conv=m_conv, mlstm_c=m_c, mlstm_n=m_n,
               mlstm_m=m_m[:, 0, 0:MLSTM_HEADS], dil1_kv=kvs[0], dil4_kv=kvs[1], dil16_kv=kvs[2])
    return x2, new


def kernel(x_prompt, x_sample, state_lru_conv, state_lru_h, state_ret, state_mlstm_conv, state_mlstm_c, state_mlstm_n, state_mlstm_m, cache_dil1_kv, cache_dil4_kv, cache_dil16_kv, w_in, lru_conv_w, lru_conv_b, lru_wa, lru_ba, lru_wx, lru_bx, lru_lambda, ret_gn_g, mlstm_conv_w, mlstm_conv_b, mlstm_wq, mlstm_wk, mlstm_w_gates, mlstm_b_gates, mlstm_skip, mlstm_gn_g, w_br_a, w_br_b, w_br_c, w_br_d, w_out, ln1_g, ln1_b, w_up, b_up, w_down, b_down, ln2_g, ln2_b):
    nb, t, d = x_prompt.shape
    ns, ts, _ = x_sample.shape
    depth = w_in.shape[0]
    row = lambda a: a.reshape(depth, 1, a.shape[-1])
    prm = dict(
        w_in_mix=w_in[:, :, 0:N_MIX_COLS].astype(bf16), w_in_gate=w_in[:, :, N_MIX_COLS:].astype(bf16), w_br_a=w_br_a.astype(bf16), w_br_b=w_br_b.astype(bf16),
        w_br_c=w_br_c.astype(bf16), w_br_d=w_br_d.astype(bf16), w_out=w_out.astype(bf16),
        w_up=w_up.astype(bf16), w_down=w_down.astype(bf16),
        lru_conv_w=lru_conv_w, lru_conv_b=row(lru_conv_b), lru_wa2=_pair_blocks(lru_wa), lru_ba=row(lru_ba),
        lru_wx2=_pair_blocks(lru_wx), lru_bx=row(lru_bx), lru_lambda=row(lru_lambda), ret_gn_g=row(ret_gn_g),
        mlstm_conv_w=mlstm_conv_w, mlstm_conv_b=row(mlstm_conv_b), mlstm_wq=mlstm_wq.astype(bf16),
        mlstm_wk=mlstm_wk.astype(bf16),
        mlstm_wg=jnp.pad(mlstm_w_gates, ((0, 0), (0, 0), (0, 128 - 2 * MLSTM_HEADS))).astype(bf16),
        mlstm_bg=row(jnp.pad(mlstm_b_gates, ((0, 0), (0, 128 - 2 * MLSTM_HEADS)))),
        mlstm_skip=row(mlstm_skip), mlstm_gn_g=row(mlstm_gn_g),
        ln1_g=row(ln1_g), ln1_b=row(ln1_b), b_up=row(b_up), b_down=row(b_down), ln2_g=row(ln2_g), ln2_b=row(ln2_b),
    )
    cos_p, sin_p = _rope_tables(jnp.arange(t, dtype=jnp.int32))
    cos_s, sin_s = _rope_tables(PAST_LEN + jnp.arange(ts, dtype=jnp.int32))
    sample_states = dict(lru_conv=state_lru_conv, lru_h=state_lru_h.reshape(depth, ns, 1, LRU_WIDTH), ret=state_ret, mlstm_conv=state_mlstm_conv,
                         mlstm_c=state_mlstm_c, mlstm_n=state_mlstm_n,
                         mlstm_m=jnp.pad(state_mlstm_m, ((0, 0), (0, 0), (0, 128 - MLSTM_HEADS)))
                         .reshape(depth, ns, 1, 128))
    caches = (cache_dil1_kv, cache_dil4_kv, cache_dil16_kv)

    xp = x_prompt.reshape(nb * t, d)
    xs = x_sample.reshape(ns * ts, d)
    yp = (xp, xp.astype(bf16))
    ys = (xs, xs.astype(bf16))
    new_p, new_s = [], []
    for layer in range(depth):
        yp, st = _stream(yp, prm, layer, nb, t, min(CHUNK, t), 256, None, None, cos_p, sin_p, f"p{layer}")
        new_p.append(st)
        ys, st = _stream(ys, prm, layer, ns, ts, min(CHUNK, ts), ts, sample_states, caches, cos_s, sin_s,
                         f"s{layer}")
        new_s.append(st)
    names = ("lru_conv", "lru_h", "ret", "mlstm_conv", "mlstm_c", "mlstm_n", "mlstm_m",
             "dil1_kv", "dil4_kv", "dil16_kv")
    outs_p = tuple(jnp.stack([s[k] for s in new_p]) for k in names)
    outs_s = tuple(jnp.stack([s[k] for s in new_s]) for k in names)
    return (yp[0].reshape(nb, t, d), ys[0].reshape(ns, ts, d)) + outs_p + outs_s
```

```python
import functools
import math

import jax
import jax.numpy as jnp
import numpy as np
from jax import lax
from jax.experimental import pallas as pl
from jax.experimental.pallas import tpu as pltpu

f32 = jnp.float32
bf16 = jnp.bfloat16

D_MODEL = 2048
DEPTH = 2
PAST_LEN = 2048
LRU_WIDTH = 512
LRU_C = 8.0
CONV_W = 4
RET_HEADS = 8
RET_DIM = 64
MLSTM_HEADS = 4
MLSTM_DIM = 128
DIL_GROUPS = ((128, 1), (512, 4), (2048, 16))
DIL_SPAN = 128
DIL_HEADS = 4
ATT_DIM = 64
ATT_SCALE = ATT_DIM ** -0.5
ROPE_THETA = 10000.0
D_FF = 4 * D_MODEL
CHUNK = 128
LN_EPS = 1e-5
GN_EPS = 1e-6
ALPHA = (2 * DEPTH) ** 0.25
N_MIX_COLS = 6912
N_GATE_COLS = 4 * D_MODEL
COL_A = 0
COL_B = 1024
COL_C = 3072
COL_DQ = 4608
COL_DK = 5376
COL_DV = 6144

TILE = 128
VMEM_LIMIT = 56 * 1024 * 1024


def _cparams(n_axes):
    return pltpu.CompilerParams(dimension_semantics=("arbitrary",) * n_axes, vmem_limit_bytes=VMEM_LIMIT)


def _dot(a, b):
    return jnp.dot(a.astype(bf16), b.astype(bf16), preferred_element_type=f32)


def _dot_nt(a, b):
    return lax.dot_general(a.astype(bf16), b.astype(bf16), (((1,), (1,)), ((), ())), preferred_element_type=f32)


def _dot_tn(a, b):
    return lax.dot_general(a.astype(bf16), b.astype(bf16), (((0,), (0,)), ((), ())), preferred_element_type=f32)


def _sigmoid(x):
    return 0.5 * jnp.tanh(0.5 * x) + 0.5


def _silu(x):
    return x * _sigmoid(x)


def _expm1(x):
    u = jnp.exp(x)
    safe = jnp.where((u == 1.0) | (x < -20.0), 0.5, u)
    return jnp.where(u == 1.0, x, jnp.where(x < -20.0, u - 1.0, (u - 1.0) * x / jnp.log(safe)))


def _gelu_tanh(x):
    return 0.5 * x * (1.0 + jnp.tanh(math.sqrt(2.0 / math.pi) * (x + 0.044715 * (x * x * x))))


def _layer_norm_rows(v, g, b):
    mu = jnp.mean(v, axis=-1, keepdims=True)
    c = v - mu
    var = jnp.mean(c * c, axis=-1, keepdims=True)
    return c * lax.rsqrt(var + LN_EPS) * g + b


def _rotate_heads(x, cos, sin_signed):
    lane = lax.broadcasted_iota(jnp.int32, (x.shape[0], 128), 1)
    low = (lane & 63) < 32
    outs = []
    for p in range(x.shape[1] // 128):
        xs = x[:, 128 * p:128 * (p + 1)]
        partner = jnp.where(low, pltpu.roll(xs, 96, 1), pltpu.roll(xs, 32, 1))
        outs.append(xs * cos + partner * sin_signed)
    return outs[0] if len(outs) == 1 else jnp.concatenate(outs, axis=1)


def _pad_rows(x, rows):
    if x.shape[0] == rows:
        return x
    return jnp.concatenate([x, jnp.zeros((rows - x.shape[0], x.shape[1]), x.dtype)], axis=0)


def _mm_kernel(x_ref, w_ref, o_ref, wbf_ref):
    @pl.when(pl.program_id(1) == 0)
    def _():
        wbf_ref[...] = w_ref[0].astype(bf16)

    o_ref[...] = jnp.dot(x_ref[...], wbf_ref[...], preferred_element_type=f32)


def _project(x_bf, w, layer, col0, ncols, tm, tn, name):
    m, k = x_bf.shape
    assert col0 % 128 == 0 and ncols % tn == 0 and m % tm == 0
    return pl.pallas_call(
        _mm_kernel,
        grid=(ncols // tn, m // tm),
        in_specs=[pl.BlockSpec((tm, k), lambda j, i: (i, 0)),
                  pl.BlockSpec((pl.Element(1), pl.Element(k), pl.Element(tn)),
                               lambda j, i: (layer, 0, pl.multiple_of(col0 + j * tn, 128)))],
        out_specs=pl.BlockSpec((tm, tn), lambda j, i: (i, j)),
        out_shape=jax.ShapeDtypeStruct((m, ncols), f32),
        scratch_shapes=[pltpu.VMEM((k, tn), bf16)],
        compiler_params=_cparams(2),
        name=name,
    )(x_bf, w)


def _lru_kernel(x_ref, conv0_ref, h0_ref, cw_ref, cb_ref, wa_ref, ba_ref, wx_ref, bx_ref, lam_ref,
                y_ref, convo_ref, ho_ref, xp_ref, a_ref, b_ref, hc_ref, *, n, tc):
    c = pl.program_id(0)

    @pl.when(c == 0)
    def _():
        xp_ref[:, 5:8, :] = conv0_ref[...]
        hc_ref[...] = h0_ref[...]

    @pl.when(c > 0)
    def _():
        xp_ref[:, 5:8, :] = xp_ref[:, tc + 5:tc + 8, :]

    x = x_ref[:, :, 0:LRU_WIDTH]
    gate = x_ref[:, :, LRU_WIDTH:2 * LRU_WIDTH]
    xp_ref[:, 8:8 + tc, :] = x
    cw = cw_ref[...]
    ac = xp_ref[:, 5:5 + tc, :] * cw[0:1, :]
    ac = ac + xp_ref[:, 6:6 + tc, :] * cw[1:2, :]
    ac = ac + xp_ref[:, 7:7 + tc, :] * cw[2:3, :]
    ac = ac + x * cw[3:4, :]
    ac = ac + cb_ref[...]

    ac2 = ac.reshape(n * tc, LRU_WIDTH)
    ra, ri = [], []
    for p in range(4):
        seg = ac2[:, 128 * p:128 * (p + 1)].astype(bf16)
        ra.append(jnp.dot(seg, wa_ref[p], preferred_element_type=f32))
        ri.append(jnp.dot(seg, wx_ref[p], preferred_element_type=f32))
    r = _sigmoid(jnp.concatenate(ra, axis=1) + ba_ref[...])
    i = _sigmoid(jnp.concatenate(ri, axis=1) + bx_ref[...])
    nl = -lam_ref[...]
    softplus = jnp.maximum(nl, 0.0) + jnp.log1p(jnp.exp(-jnp.abs(nl)))
    log_a = -LRU_C * r * softplus
    a = jnp.exp(log_a)
    bt = jnp.sqrt(-_expm1(2.0 * log_a)) * (i * ac2)
    a_ref[...] = a.reshape(n, tc, LRU_WIDTH)
    b_ref[...] = bt.reshape(n, tc, LRU_WIDTH)

    def step(t, h):
        h = a_ref[:, pl.ds(t, 1), :] * h + b_ref[:, pl.ds(t, 1), :]
        b_ref[:, pl.ds(t, 1), :] = h
        return h

    h = lax.fori_loop(0, tc, step, hc_ref[...])
    hc_ref[...] = h
    y_ref[...] = b_ref[...] * _gelu_tanh(gate)

    @pl.when(c == pl.num_programs(0) - 1)
    def _():
        ho_ref[...] = h
        convo_ref[...] = xp_ref[:, tc + 5:tc + 8, :]


def _lru_branch(proj3, conv0, h0, conv0_map, h0_map, prm, layer, tc, name):
    n, t, _ = proj3.shape
    w2 = lambda c: (layer, 0, 0)
    w3 = lambda c: (layer, 0, 0, 0)
    kern = functools.partial(_lru_kernel, n=n, tc=tc)
    return pl.pallas_call(
        kern,
        grid=(t // tc,),
        in_specs=[pl.BlockSpec((n, tc, 2 * LRU_WIDTH), lambda c: (0, c, COL_A // (2 * LRU_WIDTH))),
                  conv0_map, h0_map,
                  pl.BlockSpec((None, CONV_W, LRU_WIDTH), w2),
                  pl.BlockSpec((None, 1, LRU_WIDTH), w2),
                  pl.BlockSpec((None, 4, 128, 128), w3),
                  pl.BlockSpec((None, 1, LRU_WIDTH), w2),
                  pl.BlockSpec((None, 4, 128, 128), w3),
                  pl.BlockSpec((None, 1, LRU_WIDTH), w2),
                  pl.BlockSpec((None, 1, LRU_WIDTH), w2)],
        out_specs=[pl.BlockSpec((n, tc, LRU_WIDTH), lambda c: (0, c, 0)),
                   pl.BlockSpec((n, CONV_W - 1, LRU_WIDTH), lambda c: (0, 0, 0)),
                   pl.BlockSpec((n, 1, LRU_WIDTH), lambda c: (0, 0, 0))],
        out_shape=[jax.ShapeDtypeStruct((n, t, LRU_WIDTH), f32),
                   jax.ShapeDtypeStruct((n, CONV_W - 1, LRU_WIDTH), f32),
                   jax.ShapeDtypeStruct((n, 1, LRU_WIDTH), f32)],
        scratch_shapes=[pltpu.VMEM((n, tc + 8, LRU_WIDTH), f32),
                        pltpu.VMEM((n, tc, LRU_WIDTH), f32),
                        pltpu.VMEM((n, tc, LRU_WIDTH), f32),
                        pltpu.VMEM((n, 1, LRU_WIDTH), f32)],
        compiler_params=_cparams(1),
        name=name,
    )(proj3, conv0, h0, prm["lru_conv_w"], prm["lru_conv_b"], prm["lru_wa2"], prm["lru_ba"],
      prm["lru_wx2"], prm["lru_bx"], prm["lru_lambda"])


def _ret_kernel(qk_ref, vg_ref, cos_ref, sin_ref, s0_ref, gn_ref, y_ref, so_ref, s_ref, *, lv):
    c = pl.program_id(1)

    @pl.when(c == 0)
    def _():
        s_ref[...] = s0_ref[...]

    cos = cos_ref[...]
    sin = sin_ref[...]
    qk = _rotate_heads(qk_ref[...], cos, sin)
    q = _pad_rows(qk[:, 0:512], TILE)
    k = _pad_rows(qk[:, 512:1024] * (RET_DIM ** -0.5), TILE)
    v = _pad_rows(vg_ref[:, 0:512], TILE)
    gate = vg_ref[:, 512:1024]
    li = lax.broadcasted_iota(jnp.int32, (TILE, TILE), 0)
    mi = lax.broadcasted_iota(jnp.int32, (TILE, TILE), 1)
    rel = (li - mi).astype(f32)
    lcol = lax.broadcasted_iota(jnp.int32, (TILE, 1), 0).astype(f32)
    outs = []
    for h in range(RET_HEADS):
        log_g = math.log1p(-(2.0 ** (-5.0 - h)))
        d_in = jnp.where(rel >= 0, jnp.exp(jnp.maximum(rel, 0.0) * log_g), 0.0)
        d_q = jnp.exp((lcol + 1.0) * log_g)
        d_k = jnp.exp((lv - 1.0 - lcol) * log_g)
        d_s = math.exp(lv * log_g)
        sl = slice(RET_DIM * h, RET_DIM * (h + 1))
        qh, kh, vh = q[:, sl], k[:, sl], v[:, sl]
        s_h = s_ref[h]
        sc = _dot_nt(qh, kh) * d_in
        o = _dot(sc, vh) + _dot(qh, s_h) * d_q
        s_ref[h] = s_h * d_s + _dot_tn(kh * d_k, vh)
        mu = jnp.mean(o, axis=-1, keepdims=True)
        oc = o - mu
        var = jnp.mean(oc * oc, axis=-1, keepdims=True)
        outs.append(oc * lax.rsqrt(var + GN_EPS))
    normed = jnp.concatenate(outs, axis=1)[0:gate.shape[0], :]
    y_ref[...] = normed * gn_ref[...] * _silu(gate)

    @pl.when(c == pl.num_programs(1) - 1)
    def _():
        so_ref[...] = s_ref[...]


def _ret_branch(proj2, cos, sin, s0, s0_spec, prm, layer, n, t, lin, name):
    nchunk = t // lin
    kern = functools.partial(_ret_kernel, lv=float(lin))
    return pl.pallas_call(
        kern,
        grid=(n, nchunk),
        in_specs=[pl.BlockSpec((lin, 1024), lambda b, c: (b * nchunk + c, COL_B // 1024)),
                  pl.BlockSpec((lin, 1024), lambda b, c: (b * nchunk + c, COL_B // 1024 + 1)),
                  pl.BlockSpec((lin, 128), lambda b, c: (c, 0)),
                  pl.BlockSpec((lin, 128), lambda b, c: (c, 0)),
                  s0_spec,
                  pl.BlockSpec((None, 1, 512), lambda b, c: (layer, 0, 0))],
        out_specs=[pl.BlockSpec((lin, 512), lambda b, c: (b * nchunk + c, 0)),
                   pl.BlockSpec((None, RET_HEADS, RET_DIM, RET_DIM), lambda b, c: (b, 0, 0, 0))],
        out_shape=[jax.ShapeDtypeStruct((n * t, 512), f32),
                   jax.ShapeDtypeStruct((n, RET_HEADS, RET_DIM, RET_DIM), f32)],
        scratch_shapes=[pltpu.VMEM((RET_HEADS, RET_DIM, RET_DIM), f32)],
        compiler_params=_cparams(2),
        name=name,
    )(proj2, proj2, cos, sin, s0, prm["ret_gn_g"])


def _split3(x):
    hi = x.astype(bf16)
    r1 = x - hi.astype(f32)
    mid = r1.astype(bf16)
    lo = (r1 - mid.astype(f32)).astype(bf16)
    return hi, mid, lo


def _mlstm_kernel(qk_ref, v_ref, z_ref, conv0_ref, c0_ref, n0_ref, m0_ref, cw_ref, cb_ref, wq_ref, wk_ref,
                  wg_ref, bg_ref, skip_ref, gn_ref,
                  y_ref, convo_ref, co_ref, no_ref, mo_ref,
                  xp_ref, cs_ref, ns_ref, ms_ref, *, lin):
    c = pl.program_id(1)
    lv = lin

    @pl.when(c == 0)
    def _():
        xp_ref[5:8, :] = conv0_ref[...]
        cs_ref[...] = c0_ref[...]
        ns_ref[...] = n0_ref[...]
        ms_ref[...] = m0_ref[...]

    @pl.when(c > 0)
    def _():
        xp_ref[5:8, :] = xp_ref[lin + 5:lin + 8, :]

    x = qk_ref[...]
    xp_ref[8:8 + lin, :] = x
    cw = cw_ref[...]
    cc = xp_ref[5:5 + lin, :] * cw[0:1, :]
    cc = cc + xp_ref[6:6 + lin, :] * cw[1:2, :]
    cc = cc + xp_ref[7:7 + lin, :] * cw[2:3, :]
    cc = cc + x * cw[3:4, :]
    cc = cc + cb_ref[...]
    c_act = _silu(cc)
    vin = v_ref[...]
    cq, ck = [], []
    for h in range(MLSTM_HEADS):
        seg = c_act[:, 128 * h:128 * (h + 1)].astype(bf16)
        cq.append(jnp.dot(seg, wq_ref[h], preferred_element_type=f32))
        ck.append(jnp.dot(seg, wk_ref[h], preferred_element_type=f32))
    cq = jnp.concatenate(cq, axis=1)
    ck = jnp.concatenate(ck, axis=1)
    gpre = (jnp.dot(cq.astype(bf16), wg_ref[0:512, :], preferred_element_type=f32)
            + jnp.dot(ck.astype(bf16), wg_ref[512:1024, :], preferred_element_type=f32)
            + jnp.dot(vin.astype(bf16), wg_ref[1024:1536, :], preferred_element_type=f32)) + bg_ref[...]

    g = _pad_rows(gpre, TILE)
    q = _pad_rows(cq, TILE)
    k = _pad_rows(ck, TILE) * (MLSTM_DIM ** -0.5)
    v = _pad_rows(vin, TILE)
    lf = jnp.minimum(g, 0.0) - jnp.log1p(jnp.exp(-jnp.abs(g)))
    li = lax.broadcasted_iota(jnp.int32, (TILE, TILE), 0)
    mi = lax.broadcasted_iota(jnp.int32, (TILE, TILE), 1)
    causal = li >= mi
    tri = jnp.where(causal, 1.0, 0.0).astype(bf16)
    hi, mid, lo = _split3(lf)
    fc = (jnp.dot(tri, hi, preferred_element_type=f32) + jnp.dot(tri, mid, preferred_element_type=f32)
          + jnp.dot(tri, lo, preferred_element_type=f32))
    fct = fc.T
    gt = g.T
    rowi = lax.broadcasted_iota(jnp.int32, (TILE, 1), 0)
    valid = rowi < lv
    lane = lax.broadcasted_iota(jnp.int32, (1, 128), 1)
    m_all = ms_ref[...]
    m_next = m_all
    outs = []
    for h in range(MLSTM_HEADS):
        sl = slice(MLSTM_DIM * h, MLSTM_DIM * (h + 1))
        qh, kh, vh = q[:, sl], k[:, sl], v[:, sl]
        f_col = fc[:, 4 + h:5 + h]
        f_row = fct[4 + h:5 + h, :]
        ig_row = gt[h:h + 1, :]
        ig_col = g[:, h:h + 1]
        m_prev = m_all[:, h:h + 1]
        log_d = jnp.where(causal, f_col - f_row + ig_row, -jnp.inf)
        log_inter = f_col + m_prev
        m_t = jnp.maximum(jnp.max(log_d, axis=1, keepdims=True), log_inter)
        dm = jnp.exp(log_d - m_t)
        inter = jnp.exp(log_inter - m_t)
        c_h = cs_ref[h]
        n_h = ns_ref[h:h + 1, :]
        s = _dot_nt(qh, kh) * dm
        num = _dot(s, vh) + inter * _dot(qh, c_h)
        den = jnp.sum(s, axis=1, keepdims=True) + inter * jnp.sum(qh * n_h, axis=1, keepdims=True)
        hout = num / jnp.maximum(jnp.abs(den), jnp.exp(-m_t))
        m_new = m_t[lv - 1:lv, :]
        f_last = f_col[lv - 1:lv, :]
        wk = jnp.where(valid, jnp.exp(f_last - f_col + ig_col - m_new), 0.0)
        dec = jnp.exp(f_last + m_prev - m_new)
        kw = kh * wk
        cs_ref[h] = dec * c_h + _dot_tn(kw, vh)
        ns_ref[h:h + 1, :] = dec * n_h + jnp.sum(kw, axis=0, keepdims=True)
        m_next = jnp.where(lane == h, m_new, m_next)
        mu = jnp.mean(hout, axis=-1, keepdims=True)
        hc = hout - mu
        var = jnp.mean(hc * hc, axis=-1, keepdims=True)
        outs.append(hc * lax.rsqrt(var + GN_EPS))
    ms_ref[...] = m_next
    normed = jnp.concatenate(outs, axis=1)[0:lin, :]
    y_ref[...] = _sigmoid(z_ref[...]) * (normed * gn_ref[...] + skip_ref[...] * c_act)

    @pl.when(c == pl.num_programs(1) - 1)
    def _():
        convo_ref[...] = xp_ref[lin + 5:lin + 8, :]
        co_ref[...] = cs_ref[...]
        no_ref[...] = ns_ref[...]
        mo_ref[...] = ms_ref[...]


def _mlstm_branch(proj2, conv0, c0, n0, m0, state_specs, prm, layer, n, t, lin, name):
    nchunk = t // lin
    kern = functools.partial(_mlstm_kernel, lin=lin)
    w2 = lambda b, c: (layer, 0, 0)
    w3 = lambda b, c: (layer, 0, 0, 0)
    cb = COL_C // 512
    return pl.pallas_call(
        kern,
        grid=(n, nchunk),
        in_specs=[pl.BlockSpec((lin, 512), lambda b, c: (b * nchunk + c, cb)),
                  pl.BlockSpec((lin, 512), lambda b, c: (b * nchunk + c, cb + 1)),
                  pl.BlockSpec((lin, 512), lambda b, c: (b * nchunk + c, cb + 2)),
                  *state_specs,
                  pl.BlockSpec((None, CONV_W, 512), w2),
                  pl.BlockSpec((None, 1, 512), w2),
                  pl.BlockSpec((None, 4, 128, 128), w3),
                  pl.BlockSpec((None, 4, 128, 128), w3),
                  pl.BlockSpec((None, 1536, 128), w2),
                  pl.BlockSpec((None, 1, 128), w2),
                  pl.BlockSpec((None, 1, 512), w2),
                  pl.BlockSpec((None, 1, 512), w2)],
        out_specs=[pl.BlockSpec((lin, 512), lambda b, c: (b * nchunk + c, 0)),
                   pl.BlockSpec((None, CONV_W - 1, 512), lambda b, c: (b, 0, 0)),
                   pl.BlockSpec((None, 4, 128, 128), lambda b, c: (b, 0, 0, 0)),
                   pl.BlockSpec((None, 4, 128), lambda b, c: (b, 0, 0)),
                   pl.BlockSpec((None, 1, 128), lambda b, c: (b, 0, 0))],
        out_shape=[jax.ShapeDtypeStruct((n * t, 512), f32),
                   jax.ShapeDtypeStruct((n, CONV_W - 1, 512), f32),
                   jax.ShapeDtypeStruct((n, 4, 128, 128), f32),
                   jax.ShapeDtypeStruct((n, 4, 128), f32),
                   jax.ShapeDtypeStruct((n, 1, 128), f32)],
        scratch_shapes=[pltpu.VMEM((lin + 8, 512), f32),
                        pltpu.VMEM((4, 128, 128), f32),
                        pltpu.VMEM((4, 128), f32),
                        pltpu.VMEM((1, 128), f32)],
        compiler_params=_cparams(2),
        name=name,
    )(proj2, proj2, proj2, conv0, c0, n0, m0, prm["mlstm_conv_w"], prm["mlstm_conv_b"], prm["mlstm_wq"],
      prm["mlstm_wk"], prm["mlstm_wg"], prm["mlstm_bg"], prm["mlstm_skip"], prm["mlstm_gn_g"])


def _dil_prompt_kernel(q_ref, k_ref, v_ref, cos_ref, sin_ref, o_ref, l_ref, kv_ref,
                       nat_ref, qs_ref, kp_ref, vp_ref, os_ref, ls_ref, *, t, dil, win):
    m = t // dil
    nb = m // DIL_SPAN
    cos = cos_ref[...]
    sin = sin_ref[...]
    k = _rotate_heads(k_ref[...], cos, sin)
    v = v_ref[...]
    q = _rotate_heads(q_ref[...], cos, sin)
    kv_ref[:, 0:256] = k[t - win:t, :]
    kv_ref[:, 256:512] = v[t - win:t, :]

    def gather(x, dst_ref, row0):
        if dil == 1:
            dst_ref[row0:row0 + t, :] = x
            return
        for j in range(2):
            nat_ref[j] = x[:, 128 * j:128 * (j + 1)]
        for r in range(dil):
            for j in range(2):
                dst_ref[row0 + r * m:row0 + (r + 1) * m, 128 * j:128 * (j + 1)] = nat_ref[j, pl.ds(r, m, stride=dil), :]

    def scatter(src_ref, dst_ref):
        if dil == 1:
            dst_ref[...] = src_ref[...]
            return
        for r in range(dil):
            for j in range(2):
                nat_ref[j, pl.ds(r, m, stride=dil), :] = src_ref[r * m:(r + 1) * m, 128 * j:128 * (j + 1)]
        for j in range(2):
            dst_ref[:, 128 * j:128 * (j + 1)] = nat_ref[j]

    zeros = jnp.zeros((DIL_SPAN, 256), f32)
    kp_ref[0:DIL_SPAN, :] = zeros
    vp_ref[0:DIL_SPAN, :] = zeros
    gather(q, qs_ref, 0)
    gather(k, kp_ref, DIL_SPAN)
    gather(v, vp_ref, DIL_SPAN)
    qi = lax.broadcasted_iota(jnp.int32, (DIL_SPAN, 2 * DIL_SPAN), 0) + DIL_SPAN
    kj = lax.broadcasted_iota(jnp.int32, (DIL_SPAN, 2 * DIL_SPAN), 1)
    rel = qi - kj
    band = (rel >= 0) & (rel <= DIL_SPAN)

    def block(gb, carry):
        r0 = pl.multiple_of(gb * DIL_SPAN, DIL_SPAN)
        b = gb & (nb - 1)
        qb = qs_ref[pl.ds(r0, DIL_SPAN), :]
        kw = kp_ref[pl.ds(r0, 2 * DIL_SPAN), :]
        vw = vp_ref[pl.ds(r0, 2 * DIL_SPAN), :]
        mask = band & ((kj + b * (2 * DIL_SPAN)) >= DIL_SPAN)
        outs, lses = [], []
        for h in range(DIL_HEADS):
            sl = slice(ATT_DIM * h, ATT_DIM * (h + 1))
            s = _dot_nt(qb[:, sl], kw[:, sl]) * ATT_SCALE
            s = jnp.where(mask, s, -jnp.inf)
            mx = jnp.max(s, axis=1, keepdims=True)
            lse = mx + jnp.log(jnp.sum(jnp.exp(s - mx), axis=1, keepdims=True))
            p = jnp.exp(s - lse)
            outs.append(_dot(p, vw[:, sl]))
            lses.append(jnp.broadcast_to(lse, (DIL_SPAN, ATT_DIM)))
        os_ref[pl.ds(r0, DIL_SPAN), :] = jnp.concatenate(outs, axis=1)
        ls_ref[pl.ds(r0, DIL_SPAN), :] = jnp.concatenate(lses, axis=1)
        return carry

    lax.fori_loop(0, t // DIL_SPAN, block, 0)
    scatter(os_ref, o_ref)
    scatter(ls_ref, l_ref)


def _dil_prompt_group(proj2, cos, sin, n, t, g, name):
    win, dil = DIL_GROUPS[g]
    win = min(win, t)
    assert t % (dil * DIL_SPAN) == 0 and (t // dil // DIL_SPAN) & (t // dil // DIL_SPAN - 1) == 0
    kern = functools.partial(_dil_prompt_kernel, t=t, dil=dil, win=win)
    o, l, kv = pl.pallas_call(
        kern,
        grid=(n,),
        in_specs=[pl.BlockSpec((t, 256), lambda b: (b, COL_DQ // 256 + g)),
                  pl.BlockSpec((t, 256), lambda b: (b, COL_DK // 256 + g)),
                  pl.BlockSpec((t, 256), lambda b: (b, COL_DV // 256 + g)),
                  pl.BlockSpec((t, 128), lambda b: (0, 0)),
                  pl.BlockSpec((t, 128), lambda b: (0, 0))],
        out_specs=[pl.BlockSpec((t, 256), lambda b: (b, 0)),
                   pl.BlockSpec((t, 256), lambda b: (b, 0)),
                   pl.BlockSpec((None, win, 512), lambda b: (b, 0, 0))],
        out_shape=[jax.ShapeDtypeStruct((n * t, 256), f32),
                   jax.ShapeDtypeStruct((n * t, 256), f32),
                   jax.ShapeDtypeStruct((n, win, 512), f32)],
        scratch_shapes=[pltpu.VMEM((2, t, 128), f32),
                        pltpu.VMEM((t, 256), f32),
                        pltpu.VMEM((t + DIL_SPAN, 256), f32),
                        pltpu.VMEM((t + DIL_SPAN, 256), f32),
                        pltpu.VMEM((t, 256), f32),
                        pltpu.VMEM((t, 256), f32)],
        compiler_params=_cparams(1),
        name=name,
    )(proj2, proj2, proj2, cos, sin)
    return o, l, kv.reshape(n, win, 2, DIL_HEADS, ATT_DIM)


def _dil_sample_kernel(q_ref, k_ref, v_ref, cos_ref, sin_ref, c1_ref, c4_ref, c16_ref,
                       o1_ref, o4_ref, o16_ref, l1_ref, l4_ref, l16_ref, kv1_ref, kv4_ref, kv16_ref, *, s_new):
    cos = cos_ref[...]
    sin = sin_ref[...]
    qpos = PAST_LEN + lax.broadcasted_iota(jnp.int32, (s_new, 1), 0)
    caches = (c1_ref, c4_ref, c16_ref)
    o_refs = (o1_ref, o4_ref, o16_ref)
    l_refs = (l1_ref, l4_ref, l16_ref)
    kv_refs = (kv1_ref, kv4_ref, kv16_ref)
    ncol_n = 128
    coln = lax.broadcasted_iota(jnp.int32, (1, ncol_n), 1)
    dist_n = qpos - (PAST_LEN + coln)
    for g, (win, dil) in enumerate(DIL_GROUPS):
        sl = slice(256 * g, 256 * (g + 1))
        qg = _rotate_heads(q_ref[:, sl], cos, sin)
        kg = _rotate_heads(k_ref[:, sl], cos, sin)
        vg = v_ref[:, sl]
        kv_refs[g][:, 0:256] = kg
        kv_refs[g][:, 256:512] = vg
        lb = min(win, PAST_LEN)
        col = lax.broadcasted_iota(jnp.int32, (1, lb), 1)
        dist = qpos - (PAST_LEN - lb + col)
        ok_c = (dist >= 0) & (dist <= DIL_SPAN * dil) & ((dist & (dil - 1)) == 0)
        ok_n = (coln < s_new) & (dist_n >= 0) & (dist_n <= DIL_SPAN * dil) & ((dist_n & (dil - 1)) == 0)
        kn = _pad_rows(kg, ncol_n)
        vn = _pad_rows(vg, ncol_n)
        outs, lses = [], []
        for h in range(DIL_HEADS):
            hs = slice(ATT_DIM * h, ATT_DIM * (h + 1))
            qh = qg[:, hs]
            sc = jnp.where(ok_c, _dot(qh, caches[g][0, h]) * ATT_SCALE, -jnp.inf)
            sn = jnp.where(ok_n, _dot_nt(qh, kn[:, hs]) * ATT_SCALE, -jnp.inf)
            mx = jnp.maximum(jnp.max(sc, axis=1, keepdims=True), jnp.max(sn, axis=1, keepdims=True))
            tot = (jnp.sum(jnp.exp(sc - mx), axis=1, keepdims=True)
                   + jnp.sum(jnp.exp(sn - mx), axis=1, keepdims=True))
            lse = mx + jnp.log(tot)
            outs.append(_dot_nt(jnp.exp(sc - lse), caches[g][1, h]) + _dot(jnp.exp(sn - lse), vn[:, hs]))
            lses.append(jnp.broadcast_to(lse, (s_new, ATT_DIM)))
        o_refs[g][...] = jnp.concatenate(outs, axis=1)
        l_refs[g][...] = jnp.concatenate(lses, axis=1)


def _dil_sample(proj2, cos, sin, c1, c4, c16, layer, n, s_new, name):
    assert PAST_LEN >= DIL_GROUPS[-1][0]
    c1v, c4v, c16v = (jnp.transpose(c, (0, 1, 3, 4, 5, 2)) for c in (c1, c4, c16))
    cspec = lambda c: pl.BlockSpec((None, None, 2, DIL_HEADS, ATT_DIM, c.shape[-1]), lambda b: (layer, b, 0, 0, 0, 0))
    kern = functools.partial(_dil_sample_kernel, s_new=s_new)
    row = lambda b: (b, 0)
    outs = pl.pallas_call(
        kern,
        grid=(n,),
        in_specs=[pl.BlockSpec((s_new, 768), lambda b: (b, COL_DQ // 768)),
                  pl.BlockSpec((s_new, 768), lambda b: (b, COL_DK // 768)),
                  pl.BlockSpec((s_new, 768), lambda b: (b, COL_DV // 768)),
                  pl.BlockSpec((s_new, 128), lambda b: (0, 0)),
                  pl.BlockSpec((s_new, 128), lambda b: (0, 0)),
                  cspec(c1v), cspec(c4v), cspec(c16v)],
        out_specs=[pl.BlockSpec((s_new, 256), row)] * 6 + [pl.BlockSpec((s_new, 512), row)] * 3,
        out_shape=[jax.ShapeDtypeStruct((n * s_new, 256), f32)] * 6
        + [jax.ShapeDtypeStruct((n * s_new, 512), f32)] * 3,
        compiler_params=_cparams(1),
        name=name,
    )(proj2, proj2, proj2, cos, sin, c1v, c4v, c16v)
    o = outs[0:3]
    l = outs[3:6]
    kv = [x.reshape(n, s_new, 2, DIL_HEADS, ATT_DIM) for x in outs[6:9]]
    return o, l, kv


def _merge_kernel(ga_ref, gb_ref, gc_ref, gd_ref, ya_ref, yb_ref, yc_ref, o1_ref, o4_ref, o16_ref,
                  l1_ref, l4_ref, l16_ref, wa_ref, wb_ref, wc_ref, wd_ref, wo_ref, u_ref, g_ref, b_ref,
                  x_ref, xbf_ref, y_s):
    j = pl.program_id(1)

    @pl.when(j == 0)
    def _():
        x_ref[...] = jnp.zeros_like(x_ref)
        y_s[:, 0:512] = ya_ref[...].astype(bf16)
        y_s[:, 512:1024] = yb_ref[...].astype(bf16)
        y_s[:, 1024:1536] = yc_ref[...].astype(bf16)
        l1, l4, l16 = l1_ref[...], l4_ref[...], l16_ref[...]
        mx = jnp.maximum(jnp.maximum(l1, l4), l16)
        e1, e4, e16 = jnp.exp(l1 - mx), jnp.exp(l4 - mx), jnp.exp(l16 - mx)
        tot = e1 + e4 + e16
        yd = (e1 / tot) * o1_ref[...] + (e4 / tot) * o4_ref[...] + (e16 / tot) * o16_ref[...]
        y_s[:, 1536:1792] = yd.astype(bf16)

    merged = _sigmoid(ga_ref[...]) * jnp.dot(y_s[:, 0:512], wa_ref[...], preferred_element_type=f32)
    merged = merged + _sigmoid(gb_ref[...]) * jnp.dot(y_s[:, 512:1024], wb_ref[...], preferred_element_type=f32)
    merged = merged + _sigmoid(gc_ref[...]) * jnp.dot(y_s[:, 1024:1536], wc_ref[...], preferred_element_type=f32)
    merged = merged + _sigmoid(gd_ref[...]) * jnp.dot(y_s[:, 1536:1792], wd_ref[...], preferred_element_type=f32)
    x_ref[...] += jnp.dot(merged.astype(bf16), wo_ref[...], preferred_element_type=f32)

    @pl.when(j == pl.num_programs(1) - 1)
    def _():
        xn = _layer_norm_rows(ALPHA * u_ref[...] + x_ref[...], g_ref[...], b_ref[...])
        x_ref[...] = xn
        xbf_ref[...] = xn.astype(bf16)


def _merge(gates, ys, d_o, d_l, u, prm, layer, tm, name):
    m = u.shape[0]
    tn = 512
    nj = D_MODEL // tn
    gspec = lambda br: pl.BlockSpec((tm, tn), lambda i, j: (i, br * nj + j))
    row512 = pl.BlockSpec((tm, 512), lambda i, j: (i, 0))
    row256 = pl.BlockSpec((tm, 256), lambda i, j: (i, 0))
    wbr = lambda kdim: pl.BlockSpec((None, kdim, tn), lambda i, j: (layer, 0, j))
    vec = pl.BlockSpec((None, 1, D_MODEL), lambda i, j: (layer, 0, 0))
    return pl.pallas_call(
        _merge_kernel,
        grid=(m // tm, nj),
        in_specs=[gspec(0), gspec(1), gspec(2), gspec(3), row512, row512, row512,
                  row256, row256, row256, row256, row256, row256,
                  wbr(512), wbr(512), wbr(512), wbr(256),
                  pl.BlockSpec((None, tn, D_MODEL), lambda i, j: (layer, j, 0)),
                  pl.BlockSpec((tm, D_MODEL), lambda i, j: (i, 0), pipeline_mode=pl.Buffered(1)), vec, vec],
        out_specs=[pl.BlockSpec((tm, D_MODEL), lambda i, j: (i, 0)),
                   pl.BlockSpec((tm, D_MODEL), lambda i, j: (i, 0))],
        out_shape=[jax.ShapeDtypeStruct((m, D_MODEL), f32), jax.ShapeDtypeStruct((m, D_MODEL), bf16)],
        scratch_shapes=[pltpu.VMEM((tm, 1792), bf16)],
        compiler_params=_cparams(2),
        name=name,
    )(gates, gates, gates, gates, *ys, *d_o, *d_l, prm["w_br_a"], prm["w_br_b"], prm["w_br_c"], prm["w_br_d"],
      prm["w_out"], u, prm["ln1_g"], prm["ln1_b"])


def _ffn_kernel(xbf_ref, x_ref, wu_ref, bu_ref, wd_ref, bd_ref, g_ref, b_ref, o_ref, obf_ref):
    j = pl.program_id(1)

    @pl.when(j == 0)
    def _():
        o_ref[...] = jnp.zeros_like(o_ref)

    hid = jnp.dot(xbf_ref[...], wu_ref[...], preferred_element_type=f32) + bu_ref[...]
    hid = jnp.square(jnp.maximum(hid, 0.0))
    o_ref[...] += jnp.dot(hid.astype(bf16), wd_ref[...], preferred_element_type=f32)

    @pl.when(j == pl.num_programs(1) - 1)
    def _():
        xn = _layer_norm_rows(ALPHA * x_ref[...] + o_ref[...] + bd_ref[...], g_ref[...], b_ref[...])
        o_ref[...] = xn
        obf_ref[...] = xn.astype(bf16)


def _ffn(x, x_bf, prm, layer, tm, tc, name):
    m = x.shape[0]
    vec = pl.BlockSpec((None, 1, D_MODEL), lambda i, j: (layer, 0, 0))
    rows = pl.BlockSpec((tm, D_MODEL), lambda i, j: (i, 0))
    return pl.pallas_call(
        _ffn_kernel,
        grid=(m // tm, D_FF // tc),
        in_specs=[rows, pl.BlockSpec((tm, D_MODEL), lambda i, j: (i, 0), pipeline_mode=pl.Buffered(1)),
                  pl.BlockSpec((None, D_MODEL, tc), lambda i, j: (layer, 0, j)),
                  pl.BlockSpec((None, 1, tc), lambda i, j: (layer, 0, j)),
                  pl.BlockSpec((None, tc, D_MODEL), lambda i, j: (layer, j, 0)),
                  vec, vec, vec],
        out_specs=[rows, pl.BlockSpec((tm, D_MODEL), lambda i, j: (i, 0), pipeline_mode=pl.Buffered(1))],
        out_shape=[jax.ShapeDtypeStruct((m, D_MODEL), f32), jax.ShapeDtypeStruct((m, D_MODEL), bf16)],
        compiler_params=_cparams(2),
        name=name,
    )(x_bf, x, prm["w_up"], prm["b_up"], prm["w_down"], prm["b_down"], prm["ln2_g"], prm["ln2_b"])


def _rope_tables(pos):
    half = ATT_DIM // 2
    inv = ROPE_THETA ** (-jnp.arange(half, dtype=f32) / half)
    ang = pos.astype(f32)[:, None] * inv[None, :]
    cos, sin = jnp.cos(ang), jnp.sin(ang)
    return jnp.concatenate([cos, cos, cos, cos], axis=1), jnp.concatenate([-sin, sin, -sin, sin], axis=1)


def _pair_blocks(w):
    l = w.shape[0]
    w = w.reshape(l, 4, 2, 64, 64)
    z = jnp.zeros((l, 4, 64, 64), w.dtype)
    top = jnp.concatenate([w[:, :, 0], z], axis=-1)
    bot = jnp.concatenate([z, w[:, :, 1]], axis=-1)
    return jnp.concatenate([top, bot], axis=-2).astype(bf16)


def _stream(x, prm, layer, n, t, lin, tc_lru, states, dil_caches, cos, sin, tag):
    xf, xb = x
    m = n * t
    tm = 1024
    proj = _project(xb, prm["w_in"], layer, 0, N_MIX_COLS, tm, 1152, f"proj_mix_{tag}")
    gates = _project(xb, prm["w_in"], layer, N_MIX_COLS, N_GATE_COLS, tm, 1024, f"proj_gate_{tag}")

    if states is None:
        zc = jnp.zeros((n, CONV_W - 1, 512), f32)
        conv_a = (zc, pl.BlockSpec((n, CONV_W - 1, 512), lambda c: (0, 0, 0)))
        h_a = (jnp.zeros((n, 1, 512), f32), pl.BlockSpec((n, 1, 512), lambda c: (0, 0, 0)))
        s0 = (jnp.zeros((n, RET_HEADS, RET_DIM, RET_DIM), f32),
              pl.BlockSpec((None, RET_HEADS, RET_DIM, RET_DIM), lambda b, c: (b, 0, 0, 0)))
        mst = (zc, jnp.zeros((n, 4, 128, 128), f32), jnp.zeros((n, 4, 128), f32), jnp.zeros((n, 1, 128), f32))
        mspecs = [pl.BlockSpec((None, CONV_W - 1, 512), lambda b, c: (b, 0, 0)),
                  pl.BlockSpec((None, 4, 128, 128), lambda b, c: (b, 0, 0, 0)),
                  pl.BlockSpec((None, 4, 128), lambda b, c: (b, 0, 0)),
                  pl.BlockSpec((None, 1, 128), lambda b, c: (b, 0, 0))]
    else:
        conv_a = (states["lru_conv"], pl.BlockSpec((None, n, CONV_W - 1, 512), lambda c: (layer, 0, 0, 0)))
        h_a = (states["lru_h"], pl.BlockSpec((None, n, 1, 512), lambda c: (layer, 0, 0, 0)))
        s0 = (states["ret"], pl.BlockSpec((None, None, RET_HEADS, RET_DIM, RET_DIM),
                                          lambda b, c: (layer, b, 0, 0, 0)))
        mst = (states["mlstm_conv"], states["mlstm_c"], states["mlstm_n"], states["mlstm_m"])
        mspecs = [pl.BlockSpec((None, None, CONV_W - 1, 512), lambda b, c: (layer, b, 0, 0)),
                  pl.BlockSpec((None, None, 4, 128, 128), lambda b, c: (layer, b, 0, 0, 0)),
                  pl.BlockSpec((None, None, 4, 128), lambda b, c: (layer, b, 0, 0)),
                  pl.BlockSpec((None, None, 1, 128), lambda b, c: (layer, b, 0, 0))]

    y_a, lru_conv, lru_h = _lru_branch(proj.reshape(n, t, N_MIX_COLS), conv_a[0], h_a[0], conv_a[1], h_a[1],
                                       prm, layer, tc_lru, f"lru_{tag}")
    y_b, ret_s = _ret_branch(proj, cos, sin, s0[0], s0[1], prm, layer, n, t, lin, f"ret_{tag}")
    y_c, m_conv, m_c, m_n, m_m = _mlstm_branch(proj, *mst, mspecs, prm, layer, n, t, lin, f"mlstm_{tag}")

    if dil_caches is None:
        d_o, d_l, kvs = [], [], []
        for g in range(len(DIL_GROUPS)):
            o, l, kv = _dil_prompt_group(proj, cos, sin, n, t, g, f"dil{g}_{tag}")
            d_o.append(o)
            d_l.append(l)
            kvs.append(kv)
    else:
        d_o, d_l, kvs = _dil_sample(proj, cos, sin, *dil_caches, layer, n, t, f"dil_{tag}")

    x1, x1b = _merge(gates, (y_a.reshape(m, 512), y_b, y_c), d_o, d_l, xf, prm, layer, 512, f"merge_{tag}")
    x2 = _ffn(x1, x1b, prm, layer, 1024, 512, f"ffn_{tag}")
    new = dict(lru_conv=lru_conv, lru_h=lru_h[:, 0, :], ret=ret_s, mlstm_conv=m_conv, mlstm_c=m_c, mlstm_n=m_n,
               mlstm_m=m_m[:, 0, 0:MLSTM_HEADS], dil1_kv=kvs[0], dil4_kv=kvs[1], dil16_kv=kvs[2])
    return x2, new


def kernel(x_prompt, x_sample, state_lru_conv, state_lru_h, state_ret, state_mlstm_conv, state_mlstm_c, state_mlstm_n, state_mlstm_m, cache_dil1_kv, cache_dil4_kv, cache_dil16_kv, w_in, lru_conv_w, lru_conv_b, lru_wa, lru_ba, lru_wx, lru_bx, lru_lambda, ret_gn_g, mlstm_conv_w, mlstm_conv_b, mlstm_wq, mlstm_wk, mlstm_w_gates, mlstm_b_gates, mlstm_skip, mlstm_gn_g, w_br_a, w_br_b, w_br_c, w_br_d, w_out, ln1_g, ln1_b, w_up, b_up, w_down, b_down, ln2_g, ln2_b):
    nb, t, d = x_prompt.shape
    ns, ts, _ = x_sample.shape
    depth = w_in.shape[0]
    row = lambda a: a.reshape(depth, 1, a.shape[-1])
    prm = dict(
        w_in=w_in, w_br_a=w_br_a.astype(bf16), w_br_b=w_br_b.astype(bf16),
        w_br_c=w_br_c.astype(bf16), w_br_d=w_br_d.astype(bf16), w_out=w_out.astype(bf16),
        w_up=w_up.astype(bf16), w_down=w_down.astype(bf16),
        lru_conv_w=lru_conv_w, lru_conv_b=row(lru_conv_b), lru_wa2=_pair_blocks(lru_wa), lru_ba=row(lru_ba),
        lru_wx2=_pair_blocks(lru_wx), lru_bx=row(lru_bx), lru_lambda=row(lru_lambda), ret_gn_g=row(ret_gn_g),
        mlstm_conv_w=mlstm_conv_w, mlstm_conv_b=row(mlstm_conv_b), mlstm_wq=mlstm_wq.astype(bf16),
        mlstm_wk=mlstm_wk.astype(bf16),
        mlstm_wg=jnp.pad(mlstm_w_gates, ((0, 0), (0, 0), (0, 128 - 2 * MLSTM_HEADS))).astype(bf16),
        mlstm_bg=row(jnp.pad(mlstm_b_gates, ((0, 0), (0, 128 - 2 * MLSTM_HEADS)))),
        mlstm_skip=row(mlstm_skip), mlstm_gn_g=row(mlstm_gn_g),
        ln1_g=row(ln1_g), ln1_b=row(ln1_b), b_up=row(b_up), b_down=row(b_down), ln2_g=row(ln2_g), ln2_b=row(ln2_b),
    )
    cos_p, sin_p = _rope_tables(jnp.arange(t, dtype=jnp.int32))
    cos_s, sin_s = _rope_tables(PAST_LEN + jnp.arange(ts, dtype=jnp.int32))
    sample_states = dict(lru_conv=state_lru_conv, lru_h=state_lru_h.reshape(depth, ns, 1, LRU_WIDTH), ret=state_ret, mlstm_conv=state_mlstm_conv,
                         mlstm_c=state_mlstm_c, mlstm_n=state_mlstm_n,
                         mlstm_m=jnp.pad(state_mlstm_m, ((0, 0), (0, 0), (0, 128 - MLSTM_HEADS)))
                         .reshape(depth, ns, 1, 128))
    caches = (cache_dil1_kv, cache_dil4_kv, cache_dil16_kv)

    xp = x_prompt.reshape(nb * t, d)
    xs = x_sample.reshape(ns * ts, d)
    yp = (xp, xp.astype(bf16))
    ys = (xs, xs.astype(bf16))
    new_p, new_s = [], []
    for layer in range(depth):
        yp, st = _stream(yp, prm, layer, nb, t, min(CHUNK, t), 256, None, None, cos_p, sin_p, f"p{layer}")
        new_p.append(st)
        ys, st = _stream(ys, prm, layer, ns, ts, min(CHUNK, ts), ts, sample_states, caches, cos_s, sin_s,
                         f"s{layer}")
        new_s.append(st)
    names = ("lru_conv", "lru_h", "ret", "mlstm_conv", "mlstm_c", "mlstm_n", "mlstm_m",
             "dil1_kv", "dil4_kv", "dil16_kv")
    outs_p = tuple(jnp.stack([s[k] for s in new_p]) for k in names)
    outs_s = tuple(jnp.stack([s[k] for s in new_s]) for k in names)
    return (yp[0].reshape(nb, t, d), ys[0].reshape(ns, ts, d)) + outs_p + outs_s
```
